```python
import math
import jax
import jax.numpy as jnp
from jax import lax
import numpy as np

D_MODEL = 2048
BATCH = 2
SEQ = 16384
DEPTH = 4

GRID_W = 64
CTX_LEN = 256
EPS = 1e-6
ROPE_THETA = 10000.0
Q_BLOCK = 128
N_MOD = 9
F32 = jnp.float32

W_S5 = D_MODEL // 4
W_SSD = D_MODEL // 4
W_MLA = D_MODEL // 4
W_GQA = D_MODEL // 4

S5_GROUP = 16
S5_GROUPS = W_S5 // S5_GROUP
S5_STATE = 64

SSD_HEADDIM = 64
SSD_HEADS = W_SSD // SSD_HEADDIM
SSD_GROUPS = 2
SSD_STATE = 128
SSD_CONV = 5
SSD_CHUNK = 128
SSD_GN = SSD_GROUPS * SSD_STATE
SSD_XBC = W_SSD + 2 * SSD_GN

MLA_HEADS = 4
MLA_Q_LORA = 384
MLA_KV_LORA = 128
MLA_NOPE = 128
MLA_ROPE = 64
MLA_V = W_MLA // MLA_HEADS
MLA_QK = MLA_NOPE + MLA_ROPE

GQA_HEADS = 4
GQA_KV_HEADS = 2
GQA_HEAD_DIM = W_GQA // GQA_HEADS

D_FF = 5632

IN_S5 = W_S5
IN_SSD = W_SSD + SSD_XBC + 2 * SSD_HEADS
IN_MLA = MLA_Q_LORA + MLA_KV_LORA + MLA_ROPE
IN_GQA = (GQA_HEADS + 2 * GQA_KV_HEADS) * GQA_HEAD_DIM
OFF_SSD = IN_S5
OFF_MLA = OFF_SSD + IN_SSD
OFF_GQA = OFF_MLA + IN_MLA
D_IN = OFF_GQA + IN_GQA

kernel_name = 'hybrid_parallel_group_diffusion_trunk'


def rms_norm(x, w):
    xf = x.astype(F32)
    y = xf * lax.rsqrt(jnp.mean(xf * xf, axis=-1, keepdims=True) + EPS)
    return (y * w.astype(F32)).astype(x.dtype)


def modulate(x, w, shift, scale):
    return (rms_norm(x, w) * (1 + scale) + shift).astype(x.dtype)


def swiglu(h, w1, w3, w2):
    return (jax.nn.silu(h @ w1) * (h @ w3)) @ w2


def half_ffn(x, norm_w, shift, scale, gate, w1, w3, w2):
    h = modulate(x, norm_w, shift, scale)
    return x + (0.5 * gate * swiglu(h, w1, w3, w2)).astype(x.dtype)


def axial_rope_tables(seq_len, rot_dim):
    rows = seq_len // GRID_W
    row = jnp.repeat(jnp.arange(rows, dtype=F32), GRID_W)
    col = jnp.tile(jnp.arange(GRID_W, dtype=F32), rows)
    n_freq = rot_dim // 4
    inv_freq = ROPE_THETA ** (-jnp.arange(n_freq, dtype=F32) / n_freq)
    ang = jnp.concatenate([row[:, None] * inv_freq, col[:, None] * inv_freq], axis=-1)
    return jnp.cos(ang), jnp.sin(ang)


def apply_rope(x, cos, sin):
    x1, x2 = jnp.split(x.astype(F32), 2, axis=-1)
    c = cos[:, None, :]
    s = sin[:, None, :]
    return jnp.concatenate([x1 * c - x2 * s, x1 * s + x2 * c], axis=-1).astype(x.dtype)


def grouped_attention(q, k, v):
    b_, l_, g_, r_, dk = q.shape
    scale = dk ** -0.5
    nb = l_ // Q_BLOCK
    qb = jnp.moveaxis(q.reshape(b_, nb, Q_BLOCK, g_, r_, dk), 1, 0)

    def attend(qblk):
        s = jnp.einsum('bqgrd,bkgd->bgrqk', qblk, k, preferred_element_type=F32) * scale
        p = jax.nn.softmax(s, axis=-1).astype(v.dtype)
        return jnp.einsum('bgrqk,bkgd->bqgrd', p, v)

    out = lax.map(attend, qb)
    return jnp.moveaxis(out, 0, 1).reshape(b_, l_, g_, r_, v.shape[-1])


def dwconv_centred(x, w, b):
    k = w.shape[0]
    y = lax.conv_general_dilated(x, w[:, None, :], window_strides=(1,), padding=[(k // 2, k // 2)],
                                 dimension_numbers=('NWC', 'WIO', 'NWC'),
                                 feature_group_count=x.shape[-1])
    return y + b


def _linear_combine(e1, e2):
    a1, b1 = e1
    a2, b2 = e2
    return a1 * a2, a2 * b1 + b2


def s5_scan(u, h0, lam_re, lam_im, log_dt, b_re, b_im, c_re, c_im):
    lam = lax.complex(lam_re.astype(F32), lam_im.astype(F32))
    step = jnp.exp(log_dt.astype(F32))[:, None]
    a_bar = jnp.exp(lam * step)
    b_bar = ((a_bar - 1.0) / lam)[..., None] * lax.complex(b_re.astype(F32), b_im.astype(F32))
    uf = u.astype(F32)
    bu = lax.complex(jnp.einsum('blgh,gph->blgp', uf, jnp.real(b_bar)),
                     jnp.einsum('blgh,gph->blgp', uf, jnp.imag(b_bar)))
    bu = bu.at[:, 0].add(a_bar * h0)
    a_seq = jnp.broadcast_to(a_bar, (1, u.shape[1]) + a_bar.shape)
    _, h = lax.associative_scan(_linear_combine, (a_seq, bu), axis=1)
    y = (jnp.einsum('blgp,ghp->blgh', jnp.real(h), c_re.astype(F32))
         - jnp.einsum('blgp,ghp->blgh', jnp.imag(h), c_im.astype(F32)))
    return y, h[:, -1]


def s5_mixer(p_ctx, p_lat, lam_re, lam_im, log_dt, b_re, b_im, c_re, c_im, d_skip, glu_w, glu_b, need_ctx):
    flip = lambda t: jnp.flip(t, axis=1)
    to_groups = lambda p: p.reshape(p.shape[0], p.shape[1], S5_GROUPS, S5_GROUP)
    uc, ul = to_groups(p_ctx), to_groups(p_lat)
    fwd = (lam_re[0], lam_im[0], log_dt[0], b_re[0], b_im[0], c_re[0], c_im[0])
    bwd = (lam_re[1], lam_im[1], log_dt[1], b_re[1], b_im[1], c_re[1], c_im[1])
    h0 = jnp.zeros((p_ctx.shape[0], S5_GROUPS, S5_STATE), jnp.complex64)
    yc_f, hc_f = s5_scan(uc, h0, *fwd)
    yl_f, _ = s5_scan(ul, hc_f, *fwd)
    yc_b, hc_b = s5_scan(flip(uc), h0, *bwd)
    yl_b, _ = s5_scan(flip(ul), hc_b, *bwd)
    d = d_skip.reshape(S5_GROUPS, S5_GROUP).astype(F32)

    def finish(yf, yb, u):
        y = yf + flip(yb) + d * u.astype(F32)
        g = jax.nn.gelu(y.reshape(u.shape[0], u.shape[1], W_S5)).astype(u.dtype)
        return g * jax.nn.sigmoid(g @ glu_w + glu_b)

    y_lat = finish(yl_f, yl_b, ul)
    y_ctx = finish(yc_f, yc_b, uc) if need_ctx else None
    return y_ctx, y_lat


def segsum(a):
    t = a.shape[-1]
    cs = jnp.cumsum(a, axis=-1)
    diff = cs[..., :, None] - cs[..., None, :]
    return jnp.where(jnp.tril(jnp.ones((t, t), dtype=bool)), diff, -jnp.inf)


def ssd_scan(xs, dt, a, bm, cm, h0):
    b_, l_, nh, hp = xs.shape
    ns = bm.shape[-1]
    nc = l_ // SSD_CHUNK
    xd = (xs.astype(F32) * dt[..., None]).reshape(b_, nc, SSD_CHUNK, nh, hp)
    la = jnp.moveaxis((dt * a).reshape(b_, nc, SSD_CHUNK, nh), -1, 1)
    bc = bm.astype(F32).reshape(b_, nc, SSD_CHUNK, nh, ns)
    cc = cm.astype(F32).reshape(b_, nc, SSD_CHUNK, nh, ns)
    acs = jnp.cumsum(la, axis=-1)
    y_diag = jnp.einsum('bclhn,bcshn,bhcls,bcshp->bclhp', cc, bc, jnp.exp(segsum(la)), xd)
    states = jnp.einsum('bclhn,bhcl,bclhp->bchpn', bc, jnp.exp(acs[..., -1:] - acs), xd)

    def chunk_step(h, inp):
        s, decay = inp
        return h * decay[:, :, None, None] + s, h

    h_final, h_in = lax.scan(chunk_step, h0,
                             (jnp.moveaxis(states, 1, 0), jnp.moveaxis(jnp.exp(acs[..., -1]), 2, 0)))
    y_off = jnp.einsum('bclhn,cbhpn,bhcl->bclhp', cc, h_in, jnp.exp(acs))
    y = (y_diag + y_off).reshape(b_, l_, nh, hp)
    return y, h_final


def ssd_mixer(p_ctx, p_lat, conv_w, conv_b, dt_bias, a_log, d_skip, norm_w, need_ctx):
    flip = lambda t: jnp.flip(t, axis=1)
    a = -jnp.exp(a_log.astype(F32))
    hpg = SSD_HEADS // SSD_GROUPS

    def prep(p):
        b_, l_ = p.shape[:2]
        z = p[..., :W_SSD]
        xbc = jax.nn.silu(dwconv_centred(p[..., W_SSD:W_SSD + SSD_XBC], conv_w, conv_b))
        xs = xbc[..., :W_SSD].reshape(b_, l_, SSD_HEADS, SSD_HEADDIM)
        bm = jnp.repeat(xbc[..., W_SSD:W_SSD + SSD_GN].reshape(b_, l_, SSD_GROUPS, SSD_STATE), hpg, axis=2)
        cm = jnp.repeat(xbc[..., W_SSD + SSD_GN:].reshape(b_, l_, SSD_GROUPS, SSD_STATE), hpg, axis=2)
        dt = jax.nn.softplus(p[..., W_SSD + SSD_XBC:].astype(F32).reshape(b_, l_, 2, SSD_HEADS)
                             + dt_bias.astype(F32))
        return z, xs, bm, cm, dt

    zc, xc, bc, cc, dtc = prep(p_ctx)
    zl, xl, bl, cl, dtl = prep(p_lat)
    h0 = jnp.zeros((p_ctx.shape[0], SSD_HEADS, SSD_HEADDIM, SSD_STATE), F32)
    yc_f, hc_f = ssd_scan(xc, dtc[:, :, 0], a[0], bc, cc, h0)
    yl_f, _ = ssd_scan(xl, dtl[:, :, 0], a[0], bl, cl, hc_f)
    yc_b, hc_b = ssd_scan(flip(xc), flip(dtc[:, :, 1]), a[1], flip(bc), flip(cc), h0)
    yl_b, _ = ssd_scan(flip(xl), flip(dtl[:, :, 1]), a[1], flip(bl), flip(cl), hc_b)
    d = d_skip.astype(F32)[:, None]

    def finish(yf, yb, xs, z):
        y = yf + flip(yb) + d * xs.astype(F32)
        y = y.reshape(z.shape).astype(z.dtype) * jax.nn.silu(z)
        return rms_norm(y, norm_w)

    y_lat = finish(yl_f, yl_b, xl, zl)
    y_ctx = finish(yc_f, yc_b, xc, zc) if need_ctx else None
    return y_ctx, y_lat


def mla_mixer(p_ctx, p_lat, q_norm_w, w_qb, kv_norm_w, w_kvb, cos, sin, need_ctx):
    def queries(p, rotate):
        b_, l_ = p.shape[:2]
        q = (rms_norm(p[..., :MLA_Q_LORA], q_norm_w) @ w_qb).reshape(b_, l_, MLA_HEADS, MLA_QK)
        q_nope, q_rope = q[..., :MLA_NOPE], q[..., MLA_NOPE:]
        if rotate:
            q_rope = apply_rope(q_rope, cos, sin)
        return jnp.concatenate([q_nope, q_rope], axis=-1)[:, :, :, None, :]

    def keys_values(p, rotate):
        b_, l_ = p.shape[:2]
        ckv = rms_norm(p[..., MLA_Q_LORA:MLA_Q_LORA + MLA_KV_LORA], kv_norm_w)
        kv = (ckv @ w_kvb).reshape(b_, l_, MLA_HEADS, MLA_NOPE + MLA_V)
        k_rope = p[..., MLA_Q_LORA + MLA_KV_LORA:][:, :, None, :]
        if rotate:
            k_rope = apply_rope(k_rope, cos, sin)
        k = jnp.concatenate([kv[..., :MLA_NOPE],
                             jnp.broadcast_to(k_rope, (b_, l_, MLA_HEADS, MLA_ROPE))], axis=-1)
        return k, kv[..., MLA_NOPE:]

    kc, vc = keys_values(p_ctx, False)
    kl, vl = keys_values(p_lat, True)
    b_, l_ = p_lat.shape[:2]
    y_lat = grouped_attention(queries(p_lat, True), jnp.concatenate([kc, kl], axis=1),
                              jnp.concatenate([vc, vl], axis=1)).reshape(b_, l_, W_MLA)
    y_ctx = (grouped_attention(queries(p_ctx, False), kc, vc).reshape(p_ctx.shape[0], p_ctx.shape[1], W_MLA)
             if need_ctx else None)
    return y_ctx, y_lat


def gqa_mixer(p_ctx, p_lat, q_norm_w, k_norm_w, cos, sin, need_ctx):
    nq = GQA_HEADS * GQA_HEAD_DIM
    nk = GQA_KV_HEADS * GQA_HEAD_DIM

    def queries(p, rotate):
        b_, l_ = p.shape[:2]
        q = rms_norm(p[..., :nq].reshape(b_, l_, GQA_HEADS, GQA_HEAD_DIM), q_norm_w)
        if rotate:
            q = apply_rope(q, cos, sin)
        return q.reshape(b_, l_, GQA_KV_HEADS, GQA_HEADS // GQA_KV_HEADS, GQA_HEAD_DIM)

    def keys_values(p, rotate):
        b_, l_ = p.shape[:2]
        k = rms_norm(p[..., nq:nq + nk].reshape(b_, l_, GQA_KV_HEADS, GQA_HEAD_DIM), k_norm_w)
        if rotate:
            k = apply_rope(k, cos, sin)
        v = p[..., nq + nk:].reshape(b_, l_, GQA_KV_HEADS, GQA_HEAD_DIM)
        return k, v

    kc, vc = keys_values(p_ctx, False)
    kl, vl = keys_values(p_lat, True)
    b_, l_ = p_lat.shape[:2]
    y_lat = grouped_attention(queries(p_lat, True), jnp.concatenate([kc, kl], axis=1),
                              jnp.concatenate([vc, vl], axis=1)).reshape(b_, l_, W_GQA)
    y_ctx = (grouped_attention(queries(p_ctx, False), kc, vc).reshape(p_ctx.shape[0], p_ctx.shape[1], W_GQA)
             if need_ctx else None)
    return y_ctx, y_lat


def split_in(p):
    return p[..., :OFF_SSD], p[..., OFF_SSD:OFF_MLA], p[..., OFF_MLA:OFF_GQA], p[..., OFF_GQA:]


def setup_inputs(seed: int = 0) -> dict:
    key = jax.random.key(seed)
    ks = iter(jax.random.split(key, 48))
    nrm = lambda shape, scale: jax.random.normal(next(ks), shape, F32) * scale
    gain = lambda shape: 1.0 + nrm(shape, 0.02)
    n_idx = jnp.arange(S5_STATE, dtype=F32)
    s5_shape = (DEPTH, 2, S5_GROUPS, S5_STATE)
    dt0 = jnp.exp(jax.random.uniform(next(ks), (DEPTH, 2, SSD_HEADS), F32, math.log(1e-3), math.log(1e-1)))
    return {
        'x': nrm((BATCH, SEQ, D_MODEL), 1.0),
        'c': nrm((BATCH, D_MODEL), 1.0),
        'ctx': nrm((BATCH, CTX_LEN, D_MODEL), 1.0),
        'c_ctx': nrm((D_MODEL,), 1.0),
        'w_ada': nrm((DEPTH, D_MODEL, N_MOD * D_MODEL), 0.5 * D_MODEL ** -0.5),
        'b_ada': nrm((DEPTH, N_MOD * D_MODEL), 0.01),
        'norm_w': gain((DEPTH, 3, D_MODEL)),
        'ffn_w1': nrm((DEPTH, 2, D_MODEL, D_FF), D_MODEL ** -0.5),
        'ffn_w3': nrm((DEPTH, 2, D_MODEL, D_FF), D_MODEL ** -0.5),
        'ffn_w2': nrm((DEPTH, 2, D_FF, D_MODEL), D_FF ** -0.5),
        'w_in': nrm((DEPTH, D_MODEL, D_IN), D_MODEL ** -0.5),
        'w_out': nrm((DEPTH, D_MODEL, D_MODEL), D_MODEL ** -0.5),
        's5_lam_re': -0.5 + nrm(s5_shape, 0.01),
        's5_lam_im': math.pi * n_idx + nrm(s5_shape, 0.01),
        's5_log_dt': jax.random.uniform(next(ks), (DEPTH, 2, S5_GROUPS), F32, math.log(1e-3), math.log(1e-1)),
        's5_b_re': nrm((DEPTH, 2, S5_GROUPS, S5_STATE, S5_GROUP), (2 * S5_GROUP) ** -0.5),
        's5_b_im': nrm((DEPTH, 2, S5_GROUPS, S5_STATE, S5_GROUP), (2 * S5_GROUP) ** -0.5),
        's5_c_re': nrm((DEPTH, 2, S5_GROUPS, S5_GROUP, S5_STATE), (2 * S5_STATE) ** -0.5),
        's5_c_im': nrm((DEPTH, 2, S5_GROUPS, S5_GROUP, S5_STATE), (2 * S5_STATE) ** -0.5),
        's5_d': nrm((DEPTH, W_S5), 1.0),
        's5_glu_w': nrm((DEPTH, W_S5, W_S5), W_S5 ** -0.5),
        's5_glu_b': nrm((DEPTH, W_S5), 0.01),
        'ssd_conv_w': nrm((DEPTH, SSD_CONV, SSD_XBC), SSD_CONV ** -0.5),
        'ssd_conv_b': nrm((DEPTH, SSD_XBC), 0.01),
        'ssd_dt_bias': dt0 + jnp.log(-jnp.expm1(-dt0)),
        'ssd_a_log': jnp.log(jax.random.uniform(next(ks), (DEPTH, 2, SSD_HEADS), F32, 1.0, 16.0)),
        'ssd_d': gain((DEPTH, SSD_HEADS)),
        'ssd_norm_w': gain((DEPTH, W_SSD)),
        'mla_q_norm_w': gain((DEPTH, MLA_Q_LORA)),
        'mla_w_qb': nrm((DEPTH, MLA_Q_LORA, MLA_HEADS * MLA_QK), MLA_Q_LORA ** -0.5),
        'mla_kv_norm_w': gain((DEPTH, MLA_KV_LORA)),
        'mla_w_kvb': nrm((DEPTH, MLA_KV_LORA, MLA_HEADS * (MLA_NOPE + MLA_V)), MLA_KV_LORA ** -0.5),
        'gqa_q_norm_w': gain((DEPTH, GQA_HEAD_DIM)),
        'gqa_k_norm_w': gain((DEPTH, GQA_HEAD_DIM)),
        'norm_f': gain((D_MODEL,)),
    }


def reference(x, c, ctx, c_ctx, w_ada, b_ada, norm_w, ffn_w1, ffn_w3, ffn_w2, w_in, w_out,
              s5_lam_re, s5_lam_im, s5_log_dt, s5_b_re, s5_b_im, s5_c_re, s5_c_im, s5_d,
              s5_glu_w, s5_glu_b, ssd_conv_w, ssd_conv_b, ssd_dt_bias, ssd_a_log, ssd_d, ssd_norm_w,
              mla_q_norm_w, mla_w_qb, mla_kv_norm_w, mla_w_kvb, gqa_q_norm_w, gqa_k_norm_w, norm_f):
    seq_len = x.shape[1]
    cos_m, sin_m = axial_rope_tables(seq_len, MLA_ROPE)
    cos_g, sin_g = axial_rope_tables(seq_len, GQA_HEAD_DIM)
    act_c = jax.nn.silu(c)
    act_cc = jax.nn.silu(c_ctx)
    xc = ctx
    for l in range(DEPTH):
        need_ctx = l < DEPTH - 1
        mod = (act_c @ w_ada[l] + b_ada[l]).reshape(-1, N_MOD, 1, D_MODEL)
        mod_c = (act_cc @ w_ada[l] + b_ada[l]).reshape(N_MOD, 1, D_MODEL)

        x = half_ffn(x, norm_w[l, 0], mod[:, 0], mod[:, 1], mod[:, 2], ffn_w1[l, 0], ffn_w3[l, 0], ffn_w2[l, 0])
        xc = half_ffn(xc, norm_w[l, 0], mod_c[0], mod_c[1], mod_c[2], ffn_w1[l, 0], ffn_w3[l, 0], ffn_w2[l, 0])

        p = modulate(x, norm_w[l, 1], mod[:, 3], mod[:, 4]) @ w_in[l]
        pc = modulate(xc, norm_w[l, 1], mod_c[3], mod_c[4]) @ w_in[l]
        p_s5, p_ssd, p_mla, p_gqa = split_in(p)
        pc_s5, pc_ssd, pc_mla, pc_gqa = split_in(pc)
        yc_s5, y_s5 = s5_mixer(pc_s5, p_s5, s5_lam_re[l], s5_lam_im[l], s5_log_dt[l], s5_b_re[l], s5_b_im[l],
                               s5_c_re[l], s5_c_im[l], s5_d[l], s5_glu_w[l], s5_glu_b[l], need_ctx)
        yc_ssd, y_ssd = ssd_mixer(pc_ssd, p_ssd, ssd_conv_w[l], ssd_conv_b[l], ssd_dt_bias[l], ssd_a_log[l],
                                  ssd_d[l], ssd_norm_w[l], need_ctx)
        yc_mla, y_mla = mla_mixer(pc_mla, p_mla, mla_q_norm_w[l], mla_w_qb[l], mla_kv_norm_w[l], mla_w_kvb[l],
                                  cos_m, sin_m, need_ctx)
        yc_gqa, y_gqa = gqa_mixer(pc_gqa, p_gqa, gqa_q_norm_w[l], gqa_k_norm_w[l], cos_g, sin_g, need_ctx)
        y = jnp.concatenate([y_s5, y_ssd, y_mla, y_gqa], axis=-1) @ w_out[l]
        x = x + (mod[:, 5] * y).astype(x.dtype)

        x = half_ffn(x, norm_w[l, 2], mod[:, 6], mod[:, 7], mod[:, 8], ffn_w1[l, 1], ffn_w3[l, 1], ffn_w2[l, 1])
        if need_ctx:
            yc = jnp.concatenate([yc_s5, yc_ssd, yc_mla, yc_gqa], axis=-1) @ w_out[l]
            xc = xc + (mod_c[5] * yc).astype(xc.dtype)
            xc = half_ffn(xc, norm_w[l, 2], mod_c[6], mod_c[7], mod_c[8], ffn_w1[l, 1], ffn_w3[l, 1], ffn_w2[l, 1])
    return rms_norm(x, norm_f)
```

```python
import functools
import math

import jax
import jax.numpy as jnp
from jax import lax
from jax.experimental import pallas as pl
from jax.experimental.pallas import tpu as pltpu

F32 = jnp.float32
BF16 = jnp.bfloat16
HIGHEST = lax.Precision.HIGHEST

EPS = 1e-6
ROPE_THETA = 10000.0
GRID_W = 64
N_MOD = 9
S5_GROUP = 16
S5_STATE = 64
SSD_HEADDIM = 64
SSD_GROUPS = 2
SSD_STATE = 128
SSD_CHUNK = 128
MLA_HEADS = 4
MLA_NOPE = 128
MLA_ROPE = 64
GQA_HEADS = 4
GQA_KV_HEADS = 2

LANE = 128
SUBLANE = 8
TM = 512
TP = 256
S5_CHUNK = 16
MIB = 1024 * 1024


def _cp(sem, vmem_mib):
    return pltpu.CompilerParams(dimension_semantics=sem, vmem_limit_bytes=vmem_mib * MIB)


def _sigmoid(x):
    return 1.0 / (1.0 + jnp.exp(-x))


def _silu(x):
    return x * _sigmoid(x)


def _rms(x, w):
    return x * lax.rsqrt(jnp.mean(x * x, axis=-1, keepdims=True) + EPS) * w


def _dot(a, b):
    return jnp.dot(a, b, preferred_element_type=F32)


def _dot_nt(a, b):
    return lax.dot_general(a, b, (((1,), (1,)), ((), ())), preferred_element_type=F32)


def _dot_tn(a, b):
    return lax.dot_general(a, b, (((0,), (0,)), ((), ())), preferred_element_type=F32)


def _dot_hi(a, b):
    return jnp.dot(a, b, preferred_element_type=F32, precision=HIGHEST)


def _ada_kernel(a_ref, w_ref, b_ref, o_ref):
    act = _silu(a_ref[...]).astype(BF16)
    o_ref[...] = _dot(act, w_ref[...].astype(BF16)) + b_ref[...]


def ada_modulation(act_in, w_ada, b_ada):
    depth, d, n = w_ada.shape
    tn = 1024
    return pl.pallas_call(
        _ada_kernel,
        out_shape=jax.ShapeDtypeStruct((depth, SUBLANE, n), F32),
        grid=(depth, n // tn),
        in_specs=[pl.BlockSpec((SUBLANE, d), lambda l, j: (0, 0)),
                  pl.BlockSpec((None, d, tn), lambda l, j: (l, 0, j)),
                  pl.BlockSpec((None, 1, tn), lambda l, j: (l, 0, j))],
        out_specs=pl.BlockSpec((None, SUBLANE, tn), lambda l, j: (l, 0, j)),
        compiler_params=_cp(("parallel", "arbitrary"), 40),
        name="ada_modulation",
    )(act_in, w_ada, b_ada.reshape(depth, 1, n))


def _ffn_kernel(x_ref, nw_ref, sh_ref, sc_ref, g_ref, w1_ref, w3_ref, w2_ref, o_ref, h_scr):
    j = pl.program_id(1)

    @pl.when(j == 0)
    def _():
        x = x_ref[...]
        h = _rms(x, nw_ref[...]) * (1.0 + sc_ref[...]) + sh_ref[...]
        h_scr[...] = h.astype(BF16)
        o_ref[...] = jnp.zeros_like(o_ref)

    h = h_scr[...]
    a = _dot(h, w1_ref[...])
    b = _dot(h, w3_ref[...])
    u = (_silu(a) * b).astype(BF16)
    o_ref[...] += _dot(u, w2_ref[...])

    @pl.when(j == pl.num_programs(1) - 1)
    def _():
        o_ref[...] = x_ref[...] + 0.5 * g_ref[...] * o_ref[...]


def half_ffn(x, nblk, row_of_block, mod4, k0, norm_w, w1, w3, w2):
    t, d = x.shape
    f = w1.shape[1]
    tf = 512
    mspec = lambda k: pl.BlockSpec((None, None, 1, d), lambda i, j: (row_of_block(i), k, 0, 0))
    return pl.pallas_call(
        _ffn_kernel,
        out_shape=jax.ShapeDtypeStruct((t, d), F32),
        grid=(nblk, f // tf),
        in_specs=[pl.BlockSpec((TM, d), lambda i, j: (i, 0)),
                  pl.BlockSpec((1, d), lambda i, j: (0, 0)),
                  mspec(k0), mspec(k0 + 1), mspec(k0 + 2),
                  pl.BlockSpec((d, tf), lambda i, j: (0, j)),
                  pl.BlockSpec((d, tf), lambda i, j: (0, j)),
                  pl.BlockSpec((tf, d), lambda i, j: (j, 0))],
        out_specs=pl.BlockSpec((TM, d), lambda i, j: (i, 0)),
        scratch_shapes=[pltpu.VMEM((TM, d), BF16)],
        compiler_params=_cp(("parallel", "arbitrary"), 48),
        name="half_ffn",
    )(x, norm_w.reshape(1, d), mod4, mod4, mod4, w1, w3, w2)


def _inproj_kernel(x_ref, nw_ref, sh_ref, sc_ref, w_ref, o_ref, h_scr):
    @pl.when(pl.program_id(1) == 0)
    def _():
        h = _rms(x_ref[...], nw_ref[...]) * (1.0 + sc_ref[...]) + sh_ref[...]
        h_scr[...] = h.astype(BF16)

    o_ref[...] = _dot(h_scr[...], w_ref[...])


def in_projection(x, nblk, row_of_block, mod4, norm_w, w):
    t, d = x.shape
    n = w.shape[1]
    tn = 1280
    mspec = lambda k: pl.BlockSpec((None, None, 1, d), lambda i, j: (row_of_block(i), k, 0, 0))
    return pl.pallas_call(
        _inproj_kernel,
        out_shape=jax.ShapeDtypeStruct((t, n), F32),
        grid=(nblk, n // tn),
        in_specs=[pl.BlockSpec((TM, d), lambda i, j: (i, 0)),
                  pl.BlockSpec((1, d), lambda i, j: (0, 0)),
                  mspec(3), mspec(4),
                  pl.BlockSpec((d, tn), lambda i, j: (0, j))],
        out_specs=pl.BlockSpec((TM, tn), lambda i, j: (i, j)),
        scratch_shapes=[pltpu.VMEM((TM, d), BF16)],
        compiler_params=_cp(("parallel", "arbitrary"), 40),
        name="in_projection",
    )(x, norm_w.reshape(1, d), mod4, mod4, w)


def s5_tables(lam_re, lam_im, log_dt, b_re, b_im, c_re, c_im):
    n = S5_CHUNK
    hp = functools.partial(jnp.einsum, precision=HIGHEST)
    step = jnp.exp(log_dt.astype(F32))[..., None]
    lr, li = lam_re.astype(F32), lam_im.astype(F32)
    jj = jnp.arange(n + 1, dtype=F32)
    mag = jnp.exp(lr[..., None] * step[..., None] * jj)
    ang = li[..., None] * step[..., None] * jj
    pr, pi = mag * jnp.cos(ang), mag * jnp.sin(ang)
    ar, ai = pr[..., 1], pi[..., 1]
    den = lr * lr + li * li
    qr = ((ar - 1.0) * lr + ai * li) / den
    qi = (ai * lr - (ar - 1.0) * li) / den
    br, bi = b_re.astype(F32), b_im.astype(F32)
    bbr = qr[..., None] * br - qi[..., None] * bi
    bbi = qr[..., None] * bi + qi[..., None] * br
    cr, ci = c_re.astype(F32), c_im.astype(F32)
    k = (hp('dgop,dgpj,dgpi->dgjoi', cr, pr[..., :n], bbr) - hp('dgop,dgpj,dgpi->dgjoi', cr, pi[..., :n], bbi)
         - hp('dgop,dgpj,dgpi->dgjoi', ci, pr[..., :n], bbi) - hp('dgop,dgpj,dgpi->dgjoi', ci, pi[..., :n], bbr))
    g = k.shape[1]
    h = k.shape[-1]
    s_idx = jnp.arange(n)[:, None]
    r_idx = jnp.arange(n)[None, :]
    lag = r_idx - s_idx
    kf = jnp.where((lag >= 0)[None, :, :, None, None], k[0][:, jnp.clip(lag, 0, n - 1)], 0.0)
    kb = jnp.where((lag <= 0)[None, :, :, None, None], k[1][:, jnp.clip(-lag, 0, n - 1)], 0.0)
    toep = jnp.transpose(kf + kb, (0, 1, 4, 2, 3)).reshape(g, n * h, n * h)

    def bend(d, pw_r, pw_i):
        re = hp('gps,gpi->gsip', pw_r, bbr[d]) - hp('gps,gpi->gsip', pw_i, bbi[d])
        im = hp('gps,gpi->gsip', pw_r, bbi[d]) + hp('gps,gpi->gsip', pw_i, bbr[d])
        return re.reshape(g, n * h, -1), im.reshape(g, n * h, -1)

    bf_r, bf_i = bend(0, pr[0][..., n - 1::-1][..., :n], pi[0][..., n - 1::-1][..., :n])
    bb_r, bb_i = bend(1, pr[1][..., :n], pi[1][..., :n])

    def coff(d, pw_r, pw_i):
        re = hp('gop,gpr->gpro', cr[d], pw_r) - hp('gop,gpr->gpro', ci[d], pw_i)
        im = -(hp('gop,gpr->gpro', cr[d], pw_i) + hp('gop,gpr->gpro', ci[d], pw_r))
        return re.reshape(g, -1, n * h), im.reshape(g, -1, n * h)

    cf_r, cf_i = coff(0, pr[0][..., 1:], pi[0][..., 1:])
    cb_r, cb_i = coff(1, pr[1][..., n:0:-1], pi[1][..., n:0:-1])

    q = g // 2
    w = n * h
    p = pr.shape[2]

    def pair_diag(m):
        m = m.reshape(q, 2, m.shape[1], m.shape[2])
        z = jnp.zeros_like(m[:, 0])
        return jnp.concatenate([jnp.concatenate([m[:, 0], z], axis=2), jnp.concatenate([z, m[:, 1]], axis=2)], axis=1)

    toep_p = pair_diag(toep)
    bend_p = jnp.concatenate([pair_diag(bf_r), pair_diag(bf_i), pair_diag(bb_r), pair_diag(bb_i)], axis=2)
    coff_p = jnp.concatenate([pair_diag(cf_r), pair_diag(cf_i), pair_diag(cb_r), pair_diag(cb_i)], axis=1)
    a16 = jnp.stack([pr[0][..., n], pi[0][..., n], pr[1][..., n], pi[1][..., n]], axis=1)
    a16 = a16.reshape(q, 2, 4, p).transpose(0, 2, 1, 3).reshape(q, 4, 2 * p)
    a16 = jnp.concatenate([a16, jnp.zeros((q, 4, 2 * p), F32)], axis=1)
    return toep_p.astype(BF16), bend_p.astype(BF16), coff_p.astype(BF16), a16


def _s5_local_kernel(u_ref, toep_ref, bend_ref, y_ref, h_ref):
    u = u_ref[...]
    y_ref[...] = _dot(u, toep_ref[...])
    h_ref[...] = _dot(u, bend_ref[...])


def _s5_carry_kernel(a_ref, hl_ref, hs_ref, *, nb, n_ctx_tiles, n_lat_tiles):
    a = a_ref[...]
    afr, afi, abr, abi = a[0:1], a[1:2], a[2:3], a[3:4]
    spt = SUBLANE // nb

    def tile_pass(tile_f, tile_b, carry):
        fr, fi, br, bi = carry
        rf = pl.multiple_of(tile_f * SUBLANE, SUBLANE)
        rb = pl.multiple_of(tile_b * SUBLANE, SUBLANE)
        lf = hl_ref[pl.ds(rf, SUBLANE), 0:2 * LANE]
        lb = hl_ref[pl.ds(rb, SUBLANE), 2 * LANE:4 * LANE]
        of_r, of_i, ob_r, ob_i = [], [], [], []
        for k in range(spt):
            of_r.append(fr)
            of_i.append(fi)
            lo = k * nb
            nr = afr * fr - afi * fi + lf[lo:lo + nb, 0:LANE]
            ni = afr * fi + afi * fr + lf[lo:lo + nb, LANE:2 * LANE]
            fr, fi = nr, ni
        for k in range(spt - 1, -1, -1):
            ob_r.append(br)
            ob_i.append(bi)
            lo = k * nb
            nr = abr * br - abi * bi + lb[lo:lo + nb, 0:LANE]
            ni = abr * bi + abi * br + lb[lo:lo + nb, LANE:2 * LANE]
            br, bi = nr, ni
        hs_ref[pl.ds(rf, SUBLANE), 0:LANE] = jnp.concatenate(of_r, axis=0)
        hs_ref[pl.ds(rf, SUBLANE), LANE:2 * LANE] = jnp.concatenate(of_i, axis=0)
        hs_ref[pl.ds(rb, SUBLANE), 2 * LANE:3 * LANE] = jnp.concatenate(ob_r[::-1], axis=0)
        hs_ref[pl.ds(rb, SUBLANE), 3 * LANE:4 * LANE] = jnp.concatenate(ob_i[::-1], axis=0)
        return fr, fi, br, bi

    z = jnp.zeros((nb, LANE), F32)
    carry = (z, z, z, z)
    carry = lax.fori_loop(0, n_ctx_tiles, lambda t, c: tile_pass(t, n_ctx_tiles - 1 - t, c), carry)
    lax.fori_loop(0, n_lat_tiles,
                  lambda t, c: tile_pass(n_ctx_tiles + t, n_ctx_tiles + n_lat_tiles - 1 - t, c), carry)


def _s5_out_kernel(yi_ref, hs_ref, coff_ref, y_ref):
    y_ref[...] = yi_ref[...] + _dot(hs_ref[...].astype(BF16), coff_ref[...])


def s5_scan(p, nl, nb, l_lat, l_ctx, col_blk, tables):
    toep, bend, coff, a16 = tables
    q = toep.shape[0]
    w = 2 * S5_CHUNK * S5_GROUP
    t = p.shape[0]
    u = p[:, col_blk * w:(col_blk + 1) * w]
    c_lat, c_ctx = l_lat // S5_CHUNK, l_ctx // S5_CHUNK

    def to_chunks(v, nchunk):
        v = v.reshape(nb, nchunk, S5_CHUNK, q, 2, S5_GROUP)
        return jnp.transpose(v, (3, 1, 0, 4, 2, 5)).reshape(q, nchunk * nb, w)

    def from_chunks(v, nchunk):
        v = v.reshape(q, nchunk, nb, 2, S5_CHUNK, S5_GROUP)
        return jnp.transpose(v, (2, 1, 4, 0, 3, 5)).reshape(nb * nchunk * S5_CHUNK, w)

    uc = jnp.concatenate([to_chunks(u[nl:], c_ctx), to_chunks(u[:nl], c_lat)], axis=1).astype(BF16)
    nrow = uc.shape[1]
    yi, hl = pl.pallas_call(
        _s5_local_kernel,
        out_shape=(jax.ShapeDtypeStruct((q, nrow, w), F32), jax.ShapeDtypeStruct((nrow, q * w), F32)),
        grid=(q,),
        in_specs=[pl.BlockSpec((None, nrow, w), lambda g: (g, 0, 0)),
                  pl.BlockSpec((None, w, w), lambda g: (g, 0, 0)),
                  pl.BlockSpec((None, w, w), lambda g: (g, 0, 0))],
        out_specs=(pl.BlockSpec((None, nrow, w), lambda g: (g, 0, 0)),
                   pl.BlockSpec((nrow, w), lambda g: (0, g))),
        compiler_params=_cp(("parallel",), 40),
        name="s5_local",
    )(uc, toep, bend)
    hs = pl.pallas_call(
        functools.partial(_s5_carry_kernel, nb=nb, n_ctx_tiles=c_ctx * nb // SUBLANE,
                          n_lat_tiles=c_lat * nb // SUBLANE),
        out_shape=jax.ShapeDtypeStruct((nrow, q * w), F32),
        grid=(q,),
        in_specs=[pl.BlockSpec((None, SUBLANE, LANE), lambda g: (g, 0, 0)),
                  pl.BlockSpec((nrow, w), lambda g: (0, g))],
        out_specs=pl.BlockSpec((nrow, w), lambda g: (0, g)),
        compiler_params=_cp(("parallel",), 40),
        name="s5_carry",
    )(a16, hl)
    y = pl.pallas_call(
        _s5_out_kernel,
        out_shape=jax.ShapeDtypeStruct((q, nrow, w), F32),
        grid=(q,),
        in_specs=[pl.BlockSpec((None, nrow, w), lambda g: (g, 0, 0)),
                  pl.BlockSpec((nrow, w), lambda g: (0, g)),
                  pl.BlockSpec((None, w, w), lambda g: (g, 0, 0))],
        out_specs=pl.BlockSpec((None, nrow, w), lambda g: (g, 0, 0)),
        compiler_params=_cp(("parallel",), 40),
        name="s5_out",
    )(yi, hs, coff)
    n_ctx_rows = c_ctx * nb
    return jnp.concatenate([from_chunks(y[:, n_ctx_rows:], c_lat), from_chunks(y[:, :n_ctx_rows], c_ctx)], axis=0)


def _ssd_conv_kernel(x_ref, prev_ref, next_ref, w_ref, b_ref, o_ref, pad_scr, *, blocks_per_seq, n_lat_blocks):
    i = pl.program_id(0)
    is_lat = i < n_lat_blocks
    first = jnp.logical_or(jnp.logical_not(is_lat), i % blocks_per_seq == 0)
    last = jnp.logical_or(jnp.logical_not(is_lat), (i + 1) % blocks_per_seq == 0)
    tp = x_ref.shape[0]
    pad_scr[0:SUBLANE, :] = jnp.where(first, 0.0, prev_ref[...])
    pad_scr[SUBLANE:SUBLANE + tp, :] = x_ref[...]
    pad_scr[SUBLANE + tp:2 * SUBLANE + tp, :] = jnp.where(last, 0.0, next_ref[...])
    w = w_ref[...]
    kk = 5
    acc = b_ref[...] + w[0:1] * pad_scr[SUBLANE - kk // 2:SUBLANE - kk // 2 + tp, :]
    for k in range(1, kk):
        off = SUBLANE - kk // 2 + k
        acc = acc + w[k:k + 1] * pad_scr[off:off + tp, :]
    o_ref[...] = _silu(acc)


def ssd_conv(p, col_blk, conv_w, conv_b, l_lat, nl):
    t = p.shape[0]
    c = conv_w.shape[1]
    assert conv_w.shape[0] == 5
    r = TP // SUBLANE
    wpad = jnp.concatenate([conv_w.astype(F32), jnp.zeros((SUBLANE - conv_w.shape[0], c), F32)], axis=0)
    nsub = t // SUBLANE
    return pl.pallas_call(
        functools.partial(_ssd_conv_kernel, blocks_per_seq=l_lat // TP, n_lat_blocks=nl // TP),
        out_shape=jax.ShapeDtypeStruct((t, c), F32),
        grid=(t // TP,),
        in_specs=[pl.BlockSpec((TP, c), lambda i: (i, col_blk)),
                  pl.BlockSpec((SUBLANE, c), lambda i: (jnp.maximum(i * r - 1, 0), col_blk)),
                  pl.BlockSpec((SUBLANE, c), lambda i: (jnp.minimum((i + 1) * r, nsub - 1), col_blk)),
                  pl.BlockSpec((SUBLANE, c), lambda i: (0, 0)),
                  pl.BlockSpec((1, c), lambda i: (0, 0))],
        out_specs=pl.BlockSpec((TP, c), lambda i: (i, 0)),
        scratch_shapes=[pltpu.VMEM((TP + 2 * SUBLANE, c), F32)],
        compiler_params=_cp(("parallel",), 32),
        name="ssd_conv",
    )(p, p, p, wpad, conv_b.reshape(1, c).astype(F32))


def _softplus(x):
    return jnp.maximum(x, 0.0) + jnp.log(1.0 + jnp.exp(-jnp.abs(x)))


def _ssd_scan_kernel(*refs, reverse, finalize, lane0, nheads, hd, ns, ngroups):
    if finalize:
        (xbc_ref, dt_ref, dtt_ref, bias_ref, biast_ref, a_ref, at_ref, e_ref,
         yf_ref, z_ref, d_ref, nw_ref, o_ref, st_scr) = refs
    else:
        (xbc_ref, dt_ref, dtt_ref, bias_ref, biast_ref, a_ref, at_ref, e_ref, o_ref, st_scr) = refs
    tc = xbc_ref.shape[0]
    wx = nheads * hd
    wg = ns
    hpg = nheads // ngroups

    @pl.when(pl.program_id(1) == 0)
    def _():
        st_scr[...] = jnp.zeros_like(st_scr)

    x = xbc_ref[:, 0:wx]
    rows = lax.broadcasted_iota(jnp.int32, (tc, tc), 0)
    cols = lax.broadcasted_iota(jnp.int32, (tc, tc), 1)
    keep = (cols >= rows) if reverse else (cols <= rows)
    tri = keep.astype(F32)
    dt = _softplus(dt_ref[...] + bias_ref[...])
    la = dt * a_ref[...]
    acs = _dot_hi(tri, la)
    dtt = _softplus(dtt_ref[...] + biast_ref[...])
    lat = dtt * at_ref[...]
    acst = _dot_hi(lat, ((rows >= cols) if reverse else (rows <= cols)).astype(F32))
    e = e_ref[...]
    eacs = _dot_hi(jnp.exp(acs), e)
    edge = tc - 1 if not reverse else 0
    tot = acs[edge:edge + 1, :]
    edec = _dot_hi(jnp.exp(tot - acs), e)
    dtx = _dot_hi(dt, e)
    chunk_decay = eacs[edge:edge + 1, :]
    xd = x * dtx
    xdb = xd.astype(BF16)
    xdd = (xd * edec).astype(BF16)
    ys = []
    for g in range(ngroups):
        bg = xbc_ref[:, wx + g * wg:wx + (g + 1) * wg].astype(BF16)
        cg = xbc_ref[:, wx + ngroups * wg + g * wg:wx + ngroups * wg + (g + 1) * wg].astype(BF16)
        cb = _dot_nt(cg, bg)
        lo, hi = g * hpg * hd, (g + 1) * hpg * hd
        st = st_scr[:, lo:hi]
        y_off = _dot(cg, st.astype(BF16)) * eacs[:, lo:hi]
        st_scr[:, lo:hi] = st * chunk_decay[:, lo:hi] + _dot_tn(bg, xdd[:, lo:hi])
        yd = []
        for hh in range(hpg):
            h = g * hpg + hh
            col = acs[:, lane0 + h:lane0 + h + 1]
            row = acst[lane0 + h:lane0 + h + 1, :]
            lm = jnp.exp(jnp.where(keep, col - row, -jnp.inf))
            m = (cb * lm).astype(BF16)
            yd.append(_dot(m, xdb[:, h * hd:(h + 1) * hd]))
        ys.append(jnp.concatenate(yd, axis=1) + y_off)
    y = jnp.concatenate(ys, axis=1)
    if finalize:
        z = z_ref[...]
        y = (yf_ref[...] + y + d_ref[...] * x) * _silu(z)
        o_ref[...] = _rms(y, nw_ref[...]).astype(o_ref.dtype)
    else:
        o_ref[...] = y


def ssd_scan(xbc, p, dt_blk, z_blk, dtt, dt_bias, a_log, d_skip, norm_w, nb, l_lat, l_ctx, nl):
    t = xbc.shape[0]
    nheads = a_log.shape[1]
    hd = SSD_HEADDIM
    wx = nheads * hd
    tc = SSD_CHUNK
    nch_ctx, nch_lat = l_ctx // tc, l_lat // tc
    nch = nch_ctx + nch_lat
    a = -jnp.exp(a_log.astype(F32))
    a_row = jnp.zeros((1, LANE), F32).at[0, :2 * nheads].set(a.reshape(-1))
    bias_row = jnp.zeros((1, LANE), F32).at[0, :2 * nheads].set(dt_bias.astype(F32).reshape(-1))
    a_col = a.reshape(2 * nheads, 1)
    bias_col = dt_bias.astype(F32).reshape(2 * nheads, 1)
    d_row = jnp.repeat(d_skip.astype(F32), hd).reshape(1, wx)
    nw_row = norm_w.astype(F32).reshape(1, wx)

    def expand(direction):
        hidx = jnp.arange(LANE)[:, None] - direction * nheads
        return (hidx == (jnp.arange(wx)[None, :] // hd)).astype(F32)

    def chunk_block(reverse):
        def f(b, k):
            is_ctx = k < nch_ctx
            ctx_c = (nch_ctx - 1 - k) if reverse else k
            lat_c = (nch_lat - 1 - (k - nch_ctx)) if reverse else (k - nch_ctx)
            return jnp.where(is_ctx, (nl + b * l_ctx) // tc + ctx_c, b * nch_lat + lat_c)
        return f

    outs = []
    yf = None
    for direction, reverse in ((0, False), (1, True)):
        blk = chunk_block(reverse)
        finalize = direction == 1
        in_specs = [pl.BlockSpec((tc, xbc.shape[1]), lambda b, k: (blk(b, k), 0)),
                    pl.BlockSpec((tc, LANE), lambda b, k: (blk(b, k), dt_blk)),
                    pl.BlockSpec((2 * nheads, tc), lambda b, k: (0, blk(b, k))),
                    pl.BlockSpec((1, LANE), lambda b, k: (0, 0)),
                    pl.BlockSpec((2 * nheads, 1), lambda b, k: (0, 0)),
                    pl.BlockSpec((1, LANE), lambda b, k: (0, 0)),
                    pl.BlockSpec((2 * nheads, 1), lambda b, k: (0, 0)),
                    pl.BlockSpec((LANE, wx), lambda b, k: (0, 0))]
        args = [xbc, p, dtt, bias_row, bias_col, a_row, a_col, expand(direction)]
        if finalize:
            in_specs += [pl.BlockSpec((tc, wx), lambda b, k: (blk(b, k), 0)),
                         pl.BlockSpec((tc, wx), lambda b, k: (blk(b, k), z_blk)),
                         pl.BlockSpec((1, wx), lambda b, k: (0, 0)),
                         pl.BlockSpec((1, wx), lambda b, k: (0, 0))]
            args += [yf, p, d_row, nw_row]
        yf = pl.pallas_call(
            functools.partial(_ssd_scan_kernel, reverse=reverse, finalize=finalize, lane0=direction * nheads,
                              nheads=nheads, hd=hd, ns=SSD_STATE, ngroups=SSD_GROUPS),
            out_shape=jax.ShapeDtypeStruct((t, wx), BF16 if finalize else F32),
            grid=(nb, nch),
            in_specs=in_specs,
            out_specs=pl.BlockSpec((tc, wx), lambda b, k: (blk(b, k), 0)),
            scratch_shapes=[pltpu.VMEM((SSD_STATE, wx), F32)],
            compiler_params=_cp(("parallel", "arbitrary"), 32),
            name="ssd_scan_bwd" if reverse else "ssd_scan_fwd",
        )(*args)
    return yf


def rope_tables(l_lat, rot_dim, nb, l_ctx):
    rows = l_lat // GRID_W
    row = jnp.repeat(jnp.arange(rows, dtype=F32), GRID_W)
    col = jnp.tile(jnp.arange(GRID_W, dtype=F32), rows)
    n_freq = rot_dim // 4
    inv_freq = ROPE_THETA ** (-jnp.arange(n_freq, dtype=F32) / n_freq)
    ang = jnp.concatenate([row[:, None] * inv_freq, col[:, None] * inv_freq], axis=-1)
    cos = jnp.concatenate([jnp.tile(jnp.cos(ang), (nb, 1)), jnp.ones((nb * l_ctx, rot_dim // 2), F32)], axis=0)
    sin = jnp.concatenate([jnp.tile(jnp.sin(ang), (nb, 1)), jnp.zeros((nb * l_ctx, rot_dim // 2), F32)], axis=0)
    return cos, sin


def _mla_prep_kernel(ql_ref, kvl_ref, kr_ref, qnw_ref, kvnw_ref, wq_ref, wk_ref, wvt_ref, c_ref, s1_ref, s2_ref,
                     q_ref, k_ref, vt_ref, *, nheads, scale):
    qn = _rms(ql_ref[...], qnw_ref[...]).astype(BF16)
    q = _dot(qn, wq_ref[...])
    ckv = _rms(kvl_ref[...], kvnw_ref[...]).astype(BF16)
    kn = _dot(ckv, wk_ref[...])
    vt_ref[...] = _dot_nt(wvt_ref[...], ckv).astype(BF16)
    c, s1, s2 = c_ref[...], s1_ref[...], s2_ref[...]

    def rope(v):
        return v * c + pltpu.roll(v, 3 * LANE // 4, 1) * s1 + pltpu.roll(v, LANE // 4, 1) * s2

    kr = rope(kr_ref[...]).astype(BF16)
    for h in range(nheads):
        b0 = 2 * LANE * h
        q_ref[:, b0:b0 + LANE] = (q[:, b0:b0 + LANE] * scale).astype(BF16)
        q_ref[:, b0 + LANE:b0 + 2 * LANE] = (rope(q[:, b0 + LANE:b0 + 2 * LANE]) * scale).astype(BF16)
        k_ref[:, b0:b0 + LANE] = kn[:, LANE * h:LANE * (h + 1)].astype(BF16)
        k_ref[:, b0 + LANE:b0 + 2 * LANE] = kr


def mla_prep(p, blk_q, blk_kv, blk_kr, q_norm_w, kv_norm_w, w_qb, w_kvb, cos, sin):
    t = p.shape[0]
    nh = MLA_HEADS
    qk = MLA_NOPE + MLA_ROPE
    ql, kvl = q_norm_w.shape[0], kv_norm_w.shape[0]
    assert MLA_NOPE == LANE and MLA_ROPE == LANE // 2 and kvl == LANE
    dv = w_kvb.shape[1] // nh - MLA_NOPE
    wq = w_qb.reshape(ql, nh, qk)
    wq = jnp.concatenate([wq, jnp.zeros((ql, nh, 2 * LANE - qk), w_qb.dtype)], axis=-1).reshape(ql, nh * 2 * LANE)
    wkv = w_kvb.reshape(kvl, nh, MLA_NOPE + dv)
    wk = wkv[:, :, :MLA_NOPE].reshape(kvl, nh * MLA_NOPE)
    wvt = wkv[:, :, MLA_NOPE:].reshape(kvl, nh * dv).T
    half = MLA_ROPE // 2
    zeros = jnp.zeros((t, half), F32)
    c_tab = jnp.concatenate([cos, cos, jnp.ones((t, 2 * half), F32)], axis=1)
    s1_tab = jnp.concatenate([-sin, zeros, zeros, zeros], axis=1)
    s2_tab = jnp.concatenate([zeros, sin, zeros, zeros], axis=1)
    row = lambda w: pl.BlockSpec((TP, w), lambda i: (i, 0))
    full = lambda a: pl.BlockSpec(a.shape, lambda i: (0, 0))
    wq, wk, wvt = wq.astype(BF16), wk.astype(BF16), wvt.astype(BF16)
    qnw, kvnw = q_norm_w.reshape(1, ql).astype(F32), kv_norm_w.reshape(1, kvl).astype(F32)
    return pl.pallas_call(
        functools.partial(_mla_prep_kernel, nheads=nh, scale=float(qk) ** -0.5),
        out_shape=(jax.ShapeDtypeStruct((t, nh * 2 * LANE), BF16), jax.ShapeDtypeStruct((t, nh * 2 * LANE), BF16),
                   jax.ShapeDtypeStruct((nh * dv, t), BF16)),
        grid=(t // TP,),
        in_specs=[pl.BlockSpec((TP, ql), lambda i: (i, blk_q)),
                  pl.BlockSpec((TP, kvl), lambda i: (i, blk_kv)),
                  pl.BlockSpec((TP, LANE), lambda i: (i, blk_kr)),
                  full(qnw), full(kvnw), full(wq), full(wk), full(wvt), row(LANE), row(LANE), row(LANE)],
        out_specs=(row(nh * 2 * LANE), row(nh * 2 * LANE), pl.BlockSpec((nh * dv, TP), lambda i: (0, i))),
        compiler_params=_cp(("parallel",), 32),
        name="mla_prep",
    )(p, p, p, qnw, kvnw, wq, wk, wvt, c_tab, s1_tab, s2_tab)


def _gqa_prep_kernel(pg_ref, qnw_ref, knw_ref, c_ref, s_ref, q_ref, k_ref, vt_ref, *, nq, nkv, scale):
    c, s = c_ref[...], s_ref[...]

    def norm_rope(v, w):
        n = _rms(v, w)
        return n * c + pltpu.roll(n, LANE // 2, 1) * s

    for h in range(nq):
        q_ref[:, h * LANE:(h + 1) * LANE] = (norm_rope(pg_ref[:, h * LANE:(h + 1) * LANE], qnw_ref[...]) * scale).astype(BF16)
    for h in range(nkv):
        o = (nq + h) * LANE
        k_ref[:, h * LANE:(h + 1) * LANE] = norm_rope(pg_ref[:, o:o + LANE], knw_ref[...]).astype(BF16)
    o = (nq + nkv) * LANE
    vt_ref[...] = pg_ref[:, o:o + nkv * LANE].T.astype(BF16)


def gqa_prep(p, blk, q_norm_w, k_norm_w, cos, sin):
    t = p.shape[0]
    nq, nkv = GQA_HEADS, GQA_KV_HEADS
    hd = q_norm_w.shape[0]
    assert hd == LANE
    w = (nq + 2 * nkv) * hd
    c_tab = jnp.concatenate([cos, cos], axis=1)
    s_tab = jnp.concatenate([-sin, sin], axis=1)
    row = lambda n: pl.BlockSpec((TP, n), lambda i: (i, 0))
    one = pl.BlockSpec((1, hd), lambda i: (0, 0))
    return pl.pallas_call(
        functools.partial(_gqa_prep_kernel, nq=nq, nkv=nkv, scale=float(hd) ** -0.5),
        out_shape=(jax.ShapeDtypeStruct((t, nq * hd), BF16), jax.ShapeDtypeStruct((t, nkv * hd), BF16),
                   jax.ShapeDtypeStruct((nkv * hd, t), BF16)),
        grid=(t // TP,),
        in_specs=[pl.BlockSpec((TP, w), lambda i: (i, blk)), one, one, row(hd), row(hd)],
        out_specs=(row(nq * hd), row(nkv * hd), pl.BlockSpec((nkv * hd, TP), lambda i: (0, i))),
        compiler_params=_cp(("parallel",), 32),
        name="gqa_prep",
    )(p, q_norm_w.reshape(1, hd).astype(F32), k_norm_w.reshape(1, hd).astype(F32), c_tab, s_tab)


def _flash_kernel(*refs, rep, dk, dv, tk, n_lat, aliased):
    if aliased:
        refs = refs[1:]
    if n_lat:
        q_ref, kc_ref, vtc_ref, kl_ref, vtl_ref, o_ref, m_scr, l_scr, acc_scr = refs
    else:
        q_ref, kc_ref, vtc_ref, o_ref, m_scr, l_scr, acc_scr = refs
    for r in range(rep):
        q = q_ref[:, r * dk:(r + 1) * dk]
        st = _dot_nt(kc_ref[...], q)
        m = jnp.max(st, axis=0, keepdims=True)
        pt = jnp.exp(st - m)
        m_scr[...] = m
        l_scr[...] = jnp.sum(pt, axis=0, keepdims=True)
        acc_scr[...] = _dot(vtc_ref[...], pt.astype(BF16))
        if n_lat:
            def body(j, carry):
                off = pl.multiple_of(j * tk, tk)
                st = _dot_nt(kl_ref[pl.ds(off, tk), :], q)
                m_old = m_scr[...]
                m_new = jnp.maximum(m_old, jnp.max(st, axis=0, keepdims=True))
                alpha = jnp.exp(m_old - m_new)
                pt = jnp.exp(st - m_new)
                m_scr[...] = m_new
                l_scr[...] = alpha * l_scr[...] + jnp.sum(pt, axis=0, keepdims=True)
                acc_scr[...] = alpha * acc_scr[...] + _dot(vtl_ref[:, pl.ds(off, tk)], pt.astype(BF16))
                return carry
            lax.fori_loop(0, n_lat, body, 0)
        out = acc_scr[...] / l_scr[...]
        o_ref[:, r * dv:(r + 1) * dv] = out.T.astype(o_ref.dtype)


def attention(q, k, vt, nb, l_lat, l_ctx, nl, hkv, rep, dk, dv, need_ctx):
    t = q.shape[0]
    tq = 512 if l_lat % 512 == 0 else TP
    tk = 512 if l_lat % 512 == 0 else TP
    nq = l_lat // tq
    cb = nl // l_ctx
    scratch = lambda n: [pltpu.VMEM((1, n), F32), pltpu.VMEM((1, n), F32), pltpu.VMEM((dv, n), F32)]
    o = pl.pallas_call(
        functools.partial(_flash_kernel, rep=rep, dk=dk, dv=dv, tk=tk, n_lat=l_lat // tk, aliased=False),
        out_shape=jax.ShapeDtypeStruct((t, hkv * rep * dv), BF16),
        grid=(nb, hkv, nq),
        in_specs=[pl.BlockSpec((tq, rep * dk), lambda b, g, i: (b * nq + i, g)),
                  pl.BlockSpec((l_ctx, dk), lambda b, g, i: (cb + b, g)),
                  pl.BlockSpec((dv, l_ctx), lambda b, g, i: (g, cb + b)),
                  pl.BlockSpec((l_lat, dk), lambda b, g, i: (b, g)),
                  pl.BlockSpec((dv, l_lat), lambda b, g, i: (g, b))],
        out_specs=pl.BlockSpec((tq, rep * dv), lambda b, g, i: (b * nq + i, g)),
        scratch_shapes=scratch(tq),
        compiler_params=_cp(("parallel", "parallel", "arbitrary"), 48),
        name="attention_latent",
    )(q, k, vt, k, vt)
    if not need_ctx:
        return o
    return pl.pallas_call(
        functools.partial(_flash_kernel, rep=rep, dk=dk, dv=dv, tk=tk, n_lat=0, aliased=True),
        out_shape=jax.ShapeDtypeStruct((t, hkv * rep * dv), BF16),
        grid=(nb, hkv),
        in_specs=[pl.BlockSpec(memory_space=pl.ANY),
                  pl.BlockSpec((l_ctx, rep * dk), lambda b, g: (cb + b, g)),
                  pl.BlockSpec((l_ctx, dk), lambda b, g: (cb + b, g)),
                  pl.BlockSpec((dv, l_ctx), lambda b, g: (g, cb + b))],
        out_specs=pl.BlockSpec((l_ctx, rep * dv), lambda b, g: (cb + b, g)),
        scratch_shapes=scratch(l_ctx),
        input_output_aliases={0: 0},
        compiler_params=_cp(("parallel", "parallel"), 32),
        name="attention_context",
    )(o, q, k, vt)


def _gelu_tanh(x):
    return 0.5 * x * (1.0 + jnp.tanh(math.sqrt(2.0 / math.pi) * (x + 0.044715 * (x * x * x))))


def _outproj_kernel(x_ref, g_ref, ys5_ref, u_ref, d_ref, gw_ref, gb_ref, yssd_ref, ymla_ref, ygqa_ref, w_ref, o_ref):
    wd = ys5_ref.shape[1]
    gl = _gelu_tanh(ys5_ref[...] + d_ref[...] * u_ref[...])
    s5 = gl * _sigmoid(_dot(gl.astype(BF16), gw_ref[...]) + gb_ref[...])
    acc = _dot(s5.astype(BF16), w_ref[0:wd, :])
    acc += _dot(yssd_ref[...], w_ref[wd:2 * wd, :])
    acc += _dot(ymla_ref[...], w_ref[2 * wd:3 * wd, :])
    acc += _dot(ygqa_ref[...], w_ref[3 * wd:4 * wd, :])
    o_ref[...] = x_ref[...] + g_ref[...] * acc


def out_projection(x, nblk, row_of_block, mod4, ys5, p, u_blk, d_s5, glu_w, glu_b, yssd, ymla, ygqa, w_out):
    t, d = x.shape
    wd = ys5.shape[1]
    row = lambda n: pl.BlockSpec((TM, n), lambda i: (i, 0))
    one = lambda n: pl.BlockSpec((1, n), lambda i: (0, 0))
    return pl.pallas_call(
        _outproj_kernel,
        out_shape=jax.ShapeDtypeStruct((t, d), F32),
        grid=(nblk,),
        in_specs=[row(d),
                  pl.BlockSpec((None, None, 1, d), lambda i: (row_of_block(i), 5, 0, 0)),
                  row(wd),
                  pl.BlockSpec((TM, wd), lambda i: (i, u_blk)),
                  one(wd),
                  pl.BlockSpec((wd, wd), lambda i: (0, 0)),
                  one(wd),
                  row(wd), row(wd), row(wd),
                  pl.BlockSpec((d, d), lambda i: (0, 0))],
        out_specs=row(d),
        compiler_params=_cp(("parallel",), 48),
        name="out_projection",
    )(x, mod4, ys5, p, d_s5.reshape(1, wd).astype(F32), glu_w.astype(BF16), glu_b.reshape(1, wd).astype(F32),
      yssd, ymla, ygqa, w_out)


def _final_norm_kernel(x_ref, w_ref, o_ref):
    o_ref[...] = _rms(x_ref[...], w_ref[...])


def final_norm(x, nrows, w):
    d = x.shape[1]
    return pl.pallas_call(
        _final_norm_kernel,
        out_shape=jax.ShapeDtypeStruct((nrows, d), F32),
        grid=(nrows // TM,),
        in_specs=[pl.BlockSpec((TM, d), lambda i: (i, 0)), pl.BlockSpec((1, d), lambda i: (0, 0))],
        out_specs=pl.BlockSpec((TM, d), lambda i: (i, 0)),
        compiler_params=_cp(("parallel",), 32),
        name="final_norm",
    )(x, w.reshape(1, d).astype(F32))


COL_XBC, COL_GQA, COL_S5, COL_Z, COL_QL, COL_KVL, COL_KR, COL_DT = 0, 1024, 2048, 2560, 3072, 3456, 3584, 3712
N_IN = 3840


def _reorder_w_in(w_in, w_s5, w_ssd, xbc_w, n_dt, ql, kvl, rope, gqa_w):
    d = w_in.shape[0]
    o_ssd = w_s5
    o_mla = o_ssd + w_ssd + xbc_w + n_dt
    o_gqa = o_mla + ql + kvl + rope
    z = lambda n: jnp.zeros((d, n), w_in.dtype)
    parts = [w_in[:, o_ssd + w_ssd:o_ssd + w_ssd + xbc_w],
             w_in[:, o_gqa:o_gqa + gqa_w],
             w_in[:, 0:w_s5],
             w_in[:, o_ssd:o_ssd + w_ssd],
             w_in[:, o_mla:o_mla + ql],
             w_in[:, o_mla + ql:o_mla + ql + kvl],
             w_in[:, o_mla + ql + kvl:o_mla + ql + kvl + rope], z(LANE - rope),
             w_in[:, o_ssd + w_ssd + xbc_w:o_ssd + w_ssd + xbc_w + n_dt], z(LANE - n_dt)]
    out = jnp.concatenate(parts, axis=1)
    assert out.shape[1] == N_IN
    return out.astype(BF16)


def kernel(x, c, ctx, c_ctx, w_ada, b_ada, norm_w, ffn_w1, ffn_w3, ffn_w2, w_in, w_out, s5_lam_re, s5_lam_im, s5_log_dt, s5_b_re, s5_b_im, s5_c_re, s5_c_im, s5_d, s5_glu_w, s5_glu_b, ssd_conv_w, ssd_conv_b, ssd_dt_bias, ssd_a_log, ssd_d, ssd_norm_w, mla_q_norm_w, mla_w_qb, mla_kv_norm_w, mla_w_kvb, gqa_q_norm_w, gqa_k_norm_w, norm_f):
    nb, l_lat, d = x.shape
    l_ctx = ctx.shape[1]
    depth = w_ada.shape[0]
    nl, nc = nb * l_lat, nb * l_ctx
    t = nl + nc
    assert l_ctx == TP and nc == TM and l_lat % TM == 0 and l_lat % GRID_W == 0 and nb < SUBLANE
    assert SUBLANE % nb == 0 and (l_ctx // S5_CHUNK * nb) % SUBLANE == 0 and (l_lat // S5_CHUNK * nb) % SUBLANE == 0
    w_s5 = s5_d.shape[1]
    w_ssd = ssd_norm_w.shape[1]
    xbc_w = ssd_conv_w.shape[2]
    nheads = ssd_a_log.shape[2]
    ql, kvl = mla_q_norm_w.shape[1], mla_kv_norm_w.shape[1]
    gqa_hd = gqa_q_norm_w.shape[1]
    gqa_w = (GQA_HEADS + 2 * GQA_KV_HEADS) * gqa_hd
    assert (w_s5, w_ssd, xbc_w, ql, kvl, gqa_w) == (512, 512, 1024, 384, 128, 1024)

    xs = jnp.concatenate([x.reshape(nl, d), ctx.reshape(nc, d)], axis=0)
    act_in = jnp.concatenate([c, c_ctx[None, :], jnp.zeros((SUBLANE - nb - 1, d), F32)], axis=0)
    mod = ada_modulation(act_in, w_ada, b_ada)
    blocks_per_batch = l_lat // TM
    n_lat_blk = nl // TM
    row_of_block = lambda i: jnp.where(i < n_lat_blk, i // blocks_per_batch, nb)
    cos_m, sin_m = rope_tables(l_lat, MLA_ROPE, nb, l_ctx)
    cos_g, sin_g = rope_tables(l_lat, gqa_hd, nb, l_ctx)

    for l in range(depth):
        need_ctx = l < depth - 1
        mod4 = mod[l].reshape(SUBLANE, N_MOD, 1, d)
        nblk_all = t // TM
        nblk_tail = nblk_all if need_ctx else n_lat_blk
        w1, w3, w2 = ffn_w1[l].astype(BF16), ffn_w3[l].astype(BF16), ffn_w2[l].astype(BF16)
        xs = half_ffn(xs, nblk_all, row_of_block, mod4, 0, norm_w[l, 0], w1[0], w3[0], w2[0])
        w_in_r = _reorder_w_in(w_in[l], w_s5, w_ssd, xbc_w, 2 * nheads, ql, kvl, MLA_ROPE, gqa_w)
        p = in_projection(xs, nblk_all, row_of_block, mod4, norm_w[l, 1], w_in_r)

        tables = s5_tables(s5_lam_re[l], s5_lam_im[l], s5_log_dt[l], s5_b_re[l], s5_b_im[l], s5_c_re[l], s5_c_im[l])
        ys5 = s5_scan(p, nl, nb, l_lat, l_ctx, COL_S5 // 512, tables)

        xbc = ssd_conv(p, COL_XBC // xbc_w, ssd_conv_w[l], ssd_conv_b[l], l_lat, nl)
        dtt = p[:, COL_DT:COL_DT + 2 * nheads].T
        yssd = ssd_scan(xbc, p, COL_DT // LANE, COL_Z // 512, dtt, ssd_dt_bias[l], ssd_a_log[l], ssd_d[l],
                        ssd_norm_w[l], nb, l_lat, l_ctx, nl)

        qm, km, vtm = mla_prep(p, COL_QL // ql, COL_KVL // kvl, COL_KR // LANE, mla_q_norm_w[l], mla_kv_norm_w[l],
                               mla_w_qb[l], mla_w_kvb[l], cos_m, sin_m)
        ymla = attention(qm, km, vtm, nb, l_lat, l_ctx, nl, MLA_HEADS, 1, 2 * LANE, vtm.shape[0] // MLA_HEADS, need_ctx)
        qg, kg, vtg = gqa_prep(p, COL_GQA // gqa_w, gqa_q_norm_w[l], gqa_k_norm_w[l], cos_g, sin_g)
        ygqa = attention(qg, kg, vtg, nb, l_lat, l_ctx, nl, GQA_KV_HEADS, GQA_HEADS // GQA_KV_HEADS, gqa_hd, gqa_hd, need_ctx)

        xs = out_projection(xs, nblk_tail, row_of_block, mod4, ys5, p, COL_S5 // 512, s5_d[l], s5_glu_w[l], s5_glu_b[l],
                            yssd, ymla, ygqa, w_out[l].astype(BF16))
        xs = half_ffn(xs, nblk_tail, row_of_block, mod4, 6, norm_w[l, 2], w1[1], w3[1], w2[1])
    return final_norm(xs, nl, norm_f).reshape(nb, l_lat, d)
```

```python
import functools
import math

import jax
import jax.numpy as jnp
from jax import lax
from jax.experimental import pallas as pl
from jax.experimental.pallas import tpu as pltpu

F32 = jnp.float32
BF16 = jnp.bfloat16
HIGHEST = lax.Precision.HIGHEST

EPS = 1e-6
ROPE_THETA = 10000.0
GRID_W = 64
N_MOD = 9
S5_GROUP = 16
S5_STATE = 64
SSD_HEADDIM = 64
SSD_GROUPS = 2
SSD_STATE = 128
SSD_CHUNK = 128
MLA_HEADS = 4
MLA_NOPE = 128
MLA_ROPE = 64
GQA_HEADS = 4
GQA_KV_HEADS = 2

LANE = 128
SUBLANE = 8
TM = 512
TP = 256
S5_CHUNK = 16
MIB = 1024 * 1024
LOG2E = 1.0 / math.log(2.0)


def _cp(sem, vmem_mib):
    return pltpu.CompilerParams(dimension_semantics=sem, vmem_limit_bytes=vmem_mib * MIB)


def _sigmoid(x):
    return 1.0 / (1.0 + jnp.exp(-x))


def _silu(x):
    return x * _sigmoid(x)


def _rms(x, w):
    return x * lax.rsqrt(jnp.mean(x * x, axis=-1, keepdims=True) + EPS) * w


def _dot(a, b):
    return jnp.dot(a, b, preferred_element_type=F32)


def _dot_nt(a, b):
    return lax.dot_general(a, b, (((1,), (1,)), ((), ())), preferred_element_type=F32)


def _dot_tn(a, b):
    return lax.dot_general(a, b, (((0,), (0,)), ((), ())), preferred_element_type=F32)


def _dot_hi(a, b):
    return jnp.dot(a, b, preferred_element_type=F32, precision=HIGHEST)


def _ada_kernel(a_ref, w_ref, b_ref, o_ref):
    act = _silu(a_ref[...]).astype(BF16)
    o_ref[...] = _dot(act, w_ref[...].astype(BF16)) + b_ref[...]


def ada_modulation(act_in, w_ada, b_ada):
    depth, d, n = w_ada.shape
    tn = 1024
    return pl.pallas_call(
        _ada_kernel,
        out_shape=jax.ShapeDtypeStruct((depth, SUBLANE, n), F32),
        grid=(depth, n // tn),
        in_specs=[pl.BlockSpec((SUBLANE, d), lambda l, j: (0, 0)),
                  pl.BlockSpec((None, d, tn), lambda l, j: (l, 0, j)),
                  pl.BlockSpec((None, 1, tn), lambda l, j: (l, 0, j))],
        out_specs=pl.BlockSpec((None, SUBLANE, tn), lambda l, j: (l, 0, j)),
        compiler_params=_cp(("parallel", "arbitrary"), 40),
        name="ada_modulation",
    )(act_in, w_ada, b_ada.reshape(depth, 1, n))


def _ffn_kernel(x_ref, nw_ref, sh_ref, sc_ref, g_ref, w1_ref, w3_ref, w2_ref, o_ref, h_scr):
    j = pl.program_id(1)

    @pl.when(j == 0)
    def _():
        x = x_ref[...]
        h = _rms(x, nw_ref[...]) * (1.0 + sc_ref[...]) + sh_ref[...]
        h_scr[...] = h.astype(BF16)
        o_ref[...] = jnp.zeros_like(o_ref)

    h = h_scr[...]
    a = _dot(h, w1_ref[...])
    b = _dot(h, w3_ref[...])
    u = (_silu(a) * b).astype(BF16)
    o_ref[...] += _dot(u, w2_ref[...])

    @pl.when(j == pl.num_programs(1) - 1)
    def _():
        o_ref[...] = x_ref[...] + 0.5 * g_ref[...] * o_ref[...]


def half_ffn(x, nblk, row_of_block, mod4, k0, norm_w, w1, w3, w2):
    t, d = x.shape
    f = w1.shape[1]
    tf = 512
    mspec = lambda k: pl.BlockSpec((None, None, 1, d), lambda i, j: (row_of_block(i), k, 0, 0))
    return pl.pallas_call(
        _ffn_kernel,
        out_shape=jax.ShapeDtypeStruct((t, d), F32),
        grid=(nblk, f // tf),
        in_specs=[pl.BlockSpec((TM, d), lambda i, j: (i, 0)),
                  pl.BlockSpec((1, d), lambda i, j: (0, 0)),
                  mspec(k0), mspec(k0 + 1), mspec(k0 + 2),
                  pl.BlockSpec((d, tf), lambda i, j: (0, j)),
                  pl.BlockSpec((d, tf), lambda i, j: (0, j)),
                  pl.BlockSpec((tf, d), lambda i, j: (j, 0))],
        out_specs=pl.BlockSpec((TM, d), lambda i, j: (i, 0)),
        scratch_shapes=[pltpu.VMEM((TM, d), BF16)],
        compiler_params=_cp(("parallel", "arbitrary"), 48),
        name="half_ffn",
    )(x, norm_w.reshape(1, d), mod4, mod4, mod4, w1, w3, w2)


def _inproj_kernel(x_ref, nw_ref, sh_ref, sc_ref, w_ref, o_ref, h_scr):
    @pl.when(pl.program_id(1) == 0)
    def _():
        h = _rms(x_ref[...], nw_ref[...]) * (1.0 + sc_ref[...]) + sh_ref[...]
        h_scr[...] = h.astype(BF16)

    o_ref[...] = _dot(h_scr[...], w_ref[...])


def in_projection(x, nblk, row_of_block, mod4, norm_w, w):
    t, d = x.shape
    n = w.shape[1]
    tn = 1280
    mspec = lambda k: pl.BlockSpec((None, None, 1, d), lambda i, j: (row_of_block(i), k, 0, 0))
    return pl.pallas_call(
        _inproj_kernel,
        out_shape=jax.ShapeDtypeStruct((t, n), F32),
        grid=(nblk, n // tn),
        in_specs=[pl.BlockSpec((TM, d), lambda i, j: (i, 0)),
                  pl.BlockSpec((1, d), lambda i, j: (0, 0)),
                  mspec(3), mspec(4),
                  pl.BlockSpec((d, tn), lambda i, j: (0, j))],
        out_specs=pl.BlockSpec((TM, tn), lambda i, j: (i, j)),
        scratch_shapes=[pltpu.VMEM((TM, d), BF16)],
        compiler_params=_cp(("parallel", "arbitrary"), 40),
        name="in_projection",
    )(x, norm_w.reshape(1, d), mod4, mod4, w)


def s5_tables(lam_re, lam_im, log_dt, b_re, b_im, c_re, c_im):
    n = S5_CHUNK
    hp = functools.partial(jnp.einsum, precision=HIGHEST)
    step = jnp.exp(log_dt.astype(F32))[..., None]
    lr, li = lam_re.astype(F32), lam_im.astype(F32)
    jj = jnp.arange(n + 1, dtype=F32)
    mag = jnp.exp(lr[..., None] * step[..., None] * jj)
    ang = li[..., None] * step[..., None] * jj
    pr, pi = mag * jnp.cos(ang), mag * jnp.sin(ang)
    ar, ai = pr[..., 1], pi[..., 1]
    den = lr * lr + li * li
    qr = ((ar - 1.0) * lr + ai * li) / den
    qi = (ai * lr - (ar - 1.0) * li) / den
    br, bi = b_re.astype(F32), b_im.astype(F32)
    bbr = qr[..., None] * br - qi[..., None] * bi
    bbi = qr[..., None] * bi + qi[..., None] * br
    cr, ci = c_re.astype(F32), c_im.astype(F32)
    k = (hp('dgop,dgpj,dgpi->dgjoi', cr, pr[..., :n], bbr) - hp('dgop,dgpj,dgpi->dgjoi', cr, pi[..., :n], bbi)
         - hp('dgop,dgpj,dgpi->dgjoi', ci, pr[..., :n], bbi) - hp('dgop,dgpj,dgpi->dgjoi', ci, pi[..., :n], bbr))
    g = k.shape[1]
    h = k.shape[-1]
    s_idx = jnp.arange(n)[:, None]
    r_idx = jnp.arange(n)[None, :]
    lag = r_idx - s_idx
    kf = jnp.where((lag >= 0)[None, :, :, None, None], k[0][:, jnp.clip(lag, 0, n - 1)], 0.0)
    kb = jnp.where((lag <= 0)[None, :, :, None, None], k[1][:, jnp.clip(-lag, 0, n - 1)], 0.0)
    toep = jnp.transpose(kf + kb, (0, 1, 4, 2, 3)).reshape(g, n * h, n * h)

    def bend(d, pw_r, pw_i):
        re = hp('gps,gpi->gsip', pw_r, bbr[d]) - hp('gps,gpi->gsip', pw_i, bbi[d])
        im = hp('gps,gpi->gsip', pw_r, bbi[d]) + hp('gps,gpi->gsip', pw_i, bbr[d])
        return re.reshape(g, n * h, -1), im.reshape(g, n * h, -1)

    bf_r, bf_i = bend(0, pr[0][..., n - 1::-1][..., :n], pi[0][..., n - 1::-1][..., :n])
    bb_r, bb_i = bend(1, pr[1][..., :n], pi[1][..., :n])

    def coff(d, pw_r, pw_i):
        re = hp('gop,gpr->gpro', cr[d], pw_r) - hp('gop,gpr->gpro', ci[d], pw_i)
        im = -(hp('gop,gpr->gpro', cr[d], pw_i) + hp('gop,gpr->gpro', ci[d], pw_r))
        return re.reshape(g, -1, n * h), im.reshape(g, -1, n * h)

    cf_r, cf_i = coff(0, pr[0][..., 1:], pi[0][..., 1:])
    cb_r, cb_i = coff(1, pr[1][..., n:0:-1], pi[1][..., n:0:-1])

    q = g // 2
    w = n * h
    p = pr.shape[2]

    def pair_diag(m):
        m = m.reshape(q, 2, m.shape[1], m.shape[2])
        z = jnp.zeros_like(m[:, 0])
        return jnp.concatenate([jnp.concatenate([m[:, 0], z], axis=2), jnp.concatenate([z, m[:, 1]], axis=2)], axis=1)

    def step_major(m, axis):
        shp = m.shape
        m = m.reshape(shp[:axis] + (2, n, h) + shp[axis + 1:])
        return jnp.swapaxes(m, axis, axis + 1).reshape(shp)

    toep_p = step_major(step_major(pair_diag(toep), 1), 2)
    bend_p = jnp.concatenate([pair_diag(bf_r), pair_diag(bf_i), pair_diag(bb_r), pair_diag(bb_i)], axis=2)
    bend_p = step_major(bend_p, 1)
    coff_p = jnp.concatenate([pair_diag(cf_r), pair_diag(cf_i), pair_diag(cb_r), pair_diag(cb_i)], axis=1)
    coff_p = step_major(coff_p, 2)
    a16 = jnp.stack([pr[0][..., n], pi[0][..., n], pr[1][..., n], pi[1][..., n]], axis=1)
    a16 = a16.reshape(q, 2, 4, p).transpose(0, 2, 1, 3).reshape(q, 4, 2 * p)
    a16 = jnp.concatenate([a16, jnp.zeros((q, 4, 2 * p), F32)], axis=1)
    return toep_p.astype(BF16), bend_p.astype(BF16), coff_p.astype(BF16), a16


def _s5_local_kernel(u_ref, toep_ref, bend_ref, y_ref, h_ref):
    u = u_ref[...]
    y_ref[...] = _dot(u, toep_ref[...])
    h_ref[...] = _dot(u, bend_ref[...])


def _s5_carry_kernel(a_ref, hl_ref, hs_ref, *, nb, n_ctx_tiles, n_lat_tiles):
    a = a_ref[...]
    afr, afi, abr, abi = a[0:1], a[1:2], a[2:3], a[3:4]
    spt = SUBLANE // nb

    def tile_pass(tile_f, tile_b, carry):
        fr, fi, br, bi = carry
        rf = pl.multiple_of(tile_f * SUBLANE, SUBLANE)
        rb = pl.multiple_of(tile_b * SUBLANE, SUBLANE)
        lf = hl_ref[pl.ds(rf, SUBLANE), 0:2 * LANE]
        lb = hl_ref[pl.ds(rb, SUBLANE), 2 * LANE:4 * LANE]
        of_r, of_i, ob_r, ob_i = [], [], [], []
        for k in range(spt):
            of_r.append(fr)
            of_i.append(fi)
            lo = k * nb
            nr = afr * fr - afi * fi + lf[lo:lo + nb, 0:LANE]
            ni = afr * fi + afi * fr + lf[lo:lo + nb, LANE:2 * LANE]
            fr, fi = nr, ni
        for k in range(spt - 1, -1, -1):
            ob_r.append(br)
            ob_i.append(bi)
            lo = k * nb
            nr = abr * br - abi * bi + lb[lo:lo + nb, 0:LANE]
            ni = abr * bi + abi * br + lb[lo:lo + nb, LANE:2 * LANE]
            br, bi = nr, ni
        hs_ref[pl.ds(rf, SUBLANE), 0:LANE] = jnp.concatenate(of_r, axis=0)
        hs_ref[pl.ds(rf, SUBLANE), LANE:2 * LANE] = jnp.concatenate(of_i, axis=0)
        hs_ref[pl.ds(rb, SUBLANE), 2 * LANE:3 * LANE] = jnp.concatenate(ob_r[::-1], axis=0)
        hs_ref[pl.ds(rb, SUBLANE), 3 * LANE:4 * LANE] = jnp.concatenate(ob_i[::-1], axis=0)
        return fr, fi, br, bi

    z = jnp.zeros((nb, LANE), F32)
    carry = (z, z, z, z)
    carry = lax.fori_loop(0, n_ctx_tiles, lambda t, c: tile_pass(t, n_ctx_tiles - 1 - t, c), carry)
    lax.fori_loop(0, n_lat_tiles,
                  lambda t, c: tile_pass(n_ctx_tiles + t, n_ctx_tiles + n_lat_tiles - 1 - t, c), carry)


def _s5_out_kernel(yi_ref, hs_ref, coff_ref, y_ref):
    y_ref[...] = yi_ref[...] + _dot(hs_ref[...].astype(BF16), coff_ref[...])


def s5_scan(p, nl, nb, l_lat, l_ctx, col_blk, tables):
    toep, bend, coff, a16 = tables
    q = toep.shape[0]
    w = 2 * S5_CHUNK * S5_GROUP
    t = p.shape[0]
    u = p[:, col_blk * w:(col_blk + 1) * w].astype(BF16)
    c_lat, c_ctx = l_lat // S5_CHUNK, l_ctx // S5_CHUNK
    wp = 2 * S5_GROUP

    def to_chunks(v, nchunk):
        v = v.reshape(nb, nchunk, S5_CHUNK, q, wp)
        return jnp.transpose(v, (3, 1, 0, 2, 4)).reshape(q, nchunk * nb, w)

    def from_chunks(v, nchunk):
        v = v.reshape(q, nchunk, nb, S5_CHUNK, wp)
        return jnp.transpose(v, (2, 1, 3, 0, 4)).reshape(nb * nchunk * S5_CHUNK, w)

    uc = jnp.concatenate([to_chunks(u[nl:], c_ctx), to_chunks(u[:nl], c_lat)], axis=1)
    nrow = uc.shape[1]
    yi, hl = pl.pallas_call(
        _s5_local_kernel,
        out_shape=(jax.ShapeDtypeStruct((q, nrow, w), F32), jax.ShapeDtypeStruct((nrow, q * w), F32)),
        grid=(q,),
        in_specs=[pl.BlockSpec((None, nrow, w), lambda g: (g, 0, 0)),
                  pl.BlockSpec((None, w, w), lambda g: (g, 0, 0)),
                  pl.BlockSpec((None, w, w), lambda g: (g, 0, 0))],
        out_specs=(pl.BlockSpec((None, nrow, w), lambda g: (g, 0, 0)),
                   pl.BlockSpec((nrow, w), lambda g: (0, g))),
        compiler_params=_cp(("parallel",), 40),
        name="s5_local",
    )(uc, toep, bend)
    hs = pl.pallas_call(
        functools.partial(_s5_carry_kernel, nb=nb, n_ctx_tiles=c_ctx * nb // SUBLANE,
                          n_lat_tiles=c_lat * nb // SUBLANE),
        out_shape=jax.ShapeDtypeStruct((nrow, q * w), F32),
        grid=(q,),
        in_specs=[pl.BlockSpec((None, SUBLANE, LANE), lambda g: (g, 0, 0)),
                  pl.BlockSpec((nrow, w), lambda g: (0, g))],
        out_specs=pl.BlockSpec((nrow, w), lambda g: (0, g)),
        compiler_params=_cp(("parallel",), 40),
        name="s5_carry",
    )(a16, hl)
    y = pl.pallas_call(
        _s5_out_kernel,
        out_shape=jax.ShapeDtypeStruct((q, nrow, w), F32),
        grid=(q,),
        in_specs=[pl.BlockSpec((None, nrow, w), lambda g: (g, 0, 0)),
                  pl.BlockSpec((nrow, w), lambda g: (0, g)),
                  pl.BlockSpec((None, w, w), lambda g: (g, 0, 0))],
        out_specs=pl.BlockSpec((None, nrow, w), lambda g: (g, 0, 0)),
        compiler_params=_cp(("parallel",), 40),
        name="s5_out",
    )(yi, hs, coff)
    n_ctx_rows = c_ctx * nb
    return jnp.concatenate([from_chunks(y[:, n_ctx_rows:], c_lat), from_chunks(y[:, :n_ctx_rows], c_ctx)], axis=0)


def _ssd_conv_kernel(x_ref, prev_ref, next_ref, w_ref, b_ref, o_ref, pad_scr, *, blocks_per_seq, n_lat_blocks):
    i = pl.program_id(0)
    is_lat = i < n_lat_blocks
    first = jnp.logical_or(jnp.logical_not(is_lat), i % blocks_per_seq == 0)
    last = jnp.logical_or(jnp.logical_not(is_lat), (i + 1) % blocks_per_seq == 0)
    tp = x_ref.shape[0]
    pad_scr[0:SUBLANE, :] = jnp.where(first, 0.0, prev_ref[...])
    pad_scr[SUBLANE:SUBLANE + tp, :] = x_ref[...]
    pad_scr[SUBLANE + tp:2 * SUBLANE + tp, :] = jnp.where(last, 0.0, next_ref[...])
    w = w_ref[...]
    kk = 5
    acc = b_ref[...] + w[0:1] * pad_scr[SUBLANE - kk // 2:SUBLANE - kk // 2 + tp, :]
    for k in range(1, kk):
        off = SUBLANE - kk // 2 + k
        acc = acc + w[k:k + 1] * pad_scr[off:off + tp, :]
    o_ref[...] = _silu(acc)


def ssd_conv(p, col_blk, conv_w, conv_b, l_lat, nl):
    t = p.shape[0]
    c = conv_w.shape[1]
    assert conv_w.shape[0] == 5
    r = TP // SUBLANE
    wpad = jnp.concatenate([conv_w.astype(F32), jnp.zeros((SUBLANE - conv_w.shape[0], c), F32)], axis=0)
    nsub = t // SUBLANE
    return pl.pallas_call(
        functools.partial(_ssd_conv_kernel, blocks_per_seq=l_lat // TP, n_lat_blocks=nl // TP),
        out_shape=jax.ShapeDtypeStruct((t, c), F32),
        grid=(t // TP,),
        in_specs=[pl.BlockSpec((TP, c), lambda i: (i, col_blk)),
                  pl.BlockSpec((SUBLANE, c), lambda i: (jnp.maximum(i * r - 1, 0), col_blk)),
                  pl.BlockSpec((SUBLANE, c), lambda i: (jnp.minimum((i + 1) * r, nsub - 1), col_blk)),
                  pl.BlockSpec((SUBLANE, c), lambda i: (0, 0)),
                  pl.BlockSpec((1, c), lambda i: (0, 0))],
        out_specs=pl.BlockSpec((TP, c), lambda i: (i, 0)),
        scratch_shapes=[pltpu.VMEM((TP + 2 * SUBLANE, c), F32)],
        compiler_params=_cp(("parallel",), 32),
        name="ssd_conv",
    )(p, p, p, wpad, conv_b.reshape(1, c).astype(F32))


def _softplus(x):
    return jnp.maximum(x, 0.0) + jnp.log(1.0 + jnp.exp(-jnp.abs(x)))


def _ssd_scan_kernel(*refs, reverse, finalize, lane0, nheads, hd, ns, ngroups):
    if finalize:
        (xbc_ref, dt_ref, dtt_ref, bias_ref, biast_ref, a_ref, at_ref, e_ref,
         yf_ref, z_ref, d_ref, nw_ref, o_ref, st_scr) = refs
    else:
        (xbc_ref, dt_ref, dtt_ref, bias_ref, biast_ref, a_ref, at_ref, e_ref, o_ref, st_scr) = refs
    tc = xbc_ref.shape[0]
    wx = nheads * hd
    wg = ns
    hpg = nheads // ngroups

    @pl.when(pl.program_id(1) == 0)
    def _():
        st_scr[...] = jnp.zeros_like(st_scr)

    x = xbc_ref[:, 0:wx]
    rows = lax.broadcasted_iota(jnp.int32, (tc, tc), 0)
    cols = lax.broadcasted_iota(jnp.int32, (tc, tc), 1)
    keep = (cols >= rows) if reverse else (cols <= rows)
    tri = keep.astype(F32)
    dt = _softplus(dt_ref[...] + bias_ref[...])
    la = dt * a_ref[...]
    acs = _dot_hi(tri, la)
    dtt = _softplus(dtt_ref[...] + biast_ref[...])
    lat = dtt * at_ref[...]
    acst = _dot_hi(lat, ((rows >= cols) if reverse else (rows <= cols)).astype(F32))
    e = e_ref[...]
    eacs = _dot_hi(jnp.exp(acs), e)
    edge = tc - 1 if not reverse else 0
    tot = acs[edge:edge + 1, :]
    edec = _dot_hi(jnp.exp(tot - acs), e)
    dtx = _dot_hi(dt, e)
    chunk_decay = eacs[edge:edge + 1, :]
    xd = x * dtx
    xdb = xd.astype(BF16)
    xdd = (xd * edec).astype(BF16)
    ys = []
    for g in range(ngroups):
        bg = xbc_ref[:, wx + g * wg:wx + (g + 1) * wg].astype(BF16)
        cg = xbc_ref[:, wx + ngroups * wg + g * wg:wx + ngroups * wg + (g + 1) * wg].astype(BF16)
        cb = _dot_nt(cg, bg)
        lo, hi = g * hpg * hd, (g + 1) * hpg * hd
        st = st_scr[:, lo:hi]
        y_off = _dot(cg, st.astype(BF16)) * eacs[:, lo:hi]
        st_scr[:, lo:hi] = st * chunk_decay[:, lo:hi] + _dot_tn(bg, xdd[:, lo:hi])
        yd = []
        for hh in range(hpg):
            h = g * hpg + hh
            col = acs[:, lane0 + h:lane0 + h + 1]
            row = acst[lane0 + h:lane0 + h + 1, :]
            lm = jnp.exp(jnp.where(keep, col - row, -jnp.inf))
            m = (cb * lm).astype(BF16)
            yd.append(_dot(m, xdb[:, h * hd:(h + 1) * hd]))
        ys.append(jnp.concatenate(yd, axis=1) + y_off)
    y = jnp.concatenate(ys, axis=1)
    if finalize:
        z = z_ref[...]
        y = (yf_ref[...] + y + d_ref[...] * x) * _silu(z)
        o_ref[...] = _rms(y, nw_ref[...]).astype(o_ref.dtype)
    else:
        o_ref[...] = y


def ssd_scan(xbc, p, dt_blk, z_blk, dtt, dt_bias, a_log, d_skip, norm_w, nb, l_lat, l_ctx, nl):
    t = xbc.shape[0]
    nheads = a_log.shape[1]
    hd = SSD_HEADDIM
    wx = nheads * hd
    tc = SSD_CHUNK
    nch_ctx, nch_lat = l_ctx // tc, l_lat // tc
    nch = nch_ctx + nch_lat
    a = -jnp.exp(a_log.astype(F32))
    a_row = jnp.zeros((1, LANE), F32).at[0, :2 * nheads].set(a.reshape(-1))
    bias_row = jnp.zeros((1, LANE), F32).at[0, :2 * nheads].set(dt_bias.astype(F32).reshape(-1))
    a_col = a.reshape(2 * nheads, 1)
    bias_col = dt_bias.astype(F32).reshape(2 * nheads, 1)
    d_row = jnp.repeat(d_skip.astype(F32), hd).reshape(1, wx)
    nw_row = norm_w.astype(F32).reshape(1, wx)

    def expand(direction):
        hidx = jnp.arange(LANE)[:, None] - direction * nheads
        return (hidx == (jnp.arange(wx)[None, :] // hd)).astype(F32)

    def chunk_block(reverse):
        def f(b, k):
            is_ctx = k < nch_ctx
            ctx_c = (nch_ctx - 1 - k) if reverse else k
            lat_c = (nch_lat - 1 - (k - nch_ctx)) if reverse else (k - nch_ctx)
            return jnp.where(is_ctx, (nl + b * l_ctx) // tc + ctx_c, b * nch_lat + lat_c)
        return f

    outs = []
    yf = None
    for direction, reverse in ((0, False), (1, True)):
        blk = chunk_block(reverse)
        finalize = direction == 1
        in_specs = [pl.BlockSpec((tc, xbc.shape[1]), lambda b, k: (blk(b, k), 0)),
                    pl.BlockSpec((tc, LANE), lambda b, k: (blk(b, k), dt_blk)),
                    pl.BlockSpec((2 * nheads, tc), lambda b, k: (0, blk(b, k))),
                    pl.BlockSpec((1, LANE), lambda b, k: (0, 0)),
                    pl.BlockSpec((2 * nheads, 1), lambda b, k: (0, 0)),
                    pl.BlockSpec((1, LANE), lambda b, k: (0, 0)),
                    pl.BlockSpec((2 * nheads, 1), lambda b, k: (0, 0)),
                    pl.BlockSpec((LANE, wx), lambda b, k: (0, 0))]
        args = [xbc, p, dtt, bias_row, bias_col, a_row, a_col, expand(direction)]
        if finalize:
            in_specs += [pl.BlockSpec((tc, wx), lambda b, k: (blk(b, k), 0)),
                         pl.BlockSpec((tc, wx), lambda b, k: (blk(b, k), z_blk)),
                         pl.BlockSpec((1, wx), lambda b, k: (0, 0)),
                         pl.BlockSpec((1, wx), lambda b, k: (0, 0))]
            args += [yf, p, d_row, nw_row]
        yf = pl.pallas_call(
            functools.partial(_ssd_scan_kernel, reverse=reverse, finalize=finalize, lane0=direction * nheads,
                              nheads=nheads, hd=hd, ns=SSD_STATE, ngroups=SSD_GROUPS),
            out_shape=jax.ShapeDtypeStruct((t, wx), BF16 if finalize else F32),
            grid=(nb, nch),
            in_specs=in_specs,
            out_specs=pl.BlockSpec((tc, wx), lambda b, k: (blk(b, k), 0)),
            scratch_shapes=[pltpu.VMEM((SSD_STATE, wx), F32)],
            compiler_params=_cp(("parallel", "arbitrary"), 32),
            name="ssd_scan_bwd" if reverse else "ssd_scan_fwd",
        )(*args)
    return yf


def rope_tables(l_lat, rot_dim, nb, l_ctx):
    rows = l_lat // GRID_W
    row = jnp.repeat(jnp.arange(rows, dtype=F32), GRID_W)
    col = jnp.tile(jnp.arange(GRID_W, dtype=F32), rows)
    n_freq = rot_dim // 4
    inv_freq = ROPE_THETA ** (-jnp.arange(n_freq, dtype=F32) / n_freq)
    ang = jnp.concatenate([row[:, None] * inv_freq, col[:, None] * inv_freq], axis=-1)
    cos = jnp.concatenate([jnp.tile(jnp.cos(ang), (nb, 1)), jnp.ones((nb * l_ctx, rot_dim // 2), F32)], axis=0)
    sin = jnp.concatenate([jnp.tile(jnp.sin(ang), (nb, 1)), jnp.zeros((nb * l_ctx, rot_dim // 2), F32)], axis=0)
    return cos, sin


def _mla_prep_kernel(ql_ref, kvl_ref, kr_ref, qnw_ref, kvnw_ref, wq_ref, wk_ref, wvt_ref, c_ref, s1_ref, s2_ref,
                     q_ref, k_ref, vt_ref, *, nheads, scale):
    qn = _rms(ql_ref[...], qnw_ref[...]).astype(BF16)
    q = _dot(qn, wq_ref[...])
    ckv = _rms(kvl_ref[...], kvnw_ref[...]).astype(BF16)
    kn = _dot(ckv, wk_ref[...])
    vt_ref[...] = _dot_nt(wvt_ref[...], ckv).astype(BF16)
    c, s1, s2 = c_ref[...], s1_ref[...], s2_ref[...]

    def rope(v):
        return v * c + pltpu.roll(v, 3 * LANE // 4, 1) * s1 + pltpu.roll(v, LANE // 4, 1) * s2

    kr = rope(kr_ref[...]).astype(BF16)
    for h in range(nheads):
        b0 = 2 * LANE * h
        q_ref[:, b0:b0 + LANE] = (q[:, b0:b0 + LANE] * scale).astype(BF16)
        q_ref[:, b0 + LANE:b0 + 2 * LANE] = (rope(q[:, b0 + LANE:b0 + 2 * LANE]) * scale).astype(BF16)
        k_ref[:, b0:b0 + LANE] = kn[:, LANE * h:LANE * (h + 1)].astype(BF16)
        k_ref[:, b0 + LANE:b0 + 2 * LANE] = kr


def mla_prep(p, blk_q, blk_kv, blk_kr, q_norm_w, kv_norm_w, w_qb, w_kvb, cos, sin):
    t = p.shape[0]
    nh = MLA_HEADS
    qk = MLA_NOPE + MLA_ROPE
    ql, kvl = q_norm_w.shape[0], kv_norm_w.shape[0]
    assert MLA_NOPE == LANE and MLA_ROPE == LANE // 2 and kvl == LANE
    dv = w_kvb.shape[1] // nh - MLA_NOPE
    wq = w_qb.reshape(ql, nh, qk)
    wq = jnp.concatenate([wq, jnp.zeros((ql, nh, 2 * LANE - qk), w_qb.dtype)], axis=-1).reshape(ql, nh * 2 * LANE)
    wkv = w_kvb.reshape(kvl, nh, MLA_NOPE + dv)
    wk = wkv[:, :, :MLA_NOPE].reshape(kvl, nh * MLA_NOPE)
    wvt = wkv[:, :, MLA_NOPE:].reshape(kvl, nh * dv).T
    half = MLA_ROPE // 2
    zeros = jnp.zeros((t, half), F32)
    c_tab = jnp.concatenate([cos, cos, jnp.ones((t, 2 * half), F32)], axis=1)
    s1_tab = jnp.concatenate([-sin, zeros, zeros, zeros], axis=1)
    s2_tab = jnp.concatenate([zeros, sin, zeros, zeros], axis=1)
    row = lambda w: pl.BlockSpec((TP, w), lambda i: (i, 0))
    full = lambda a: pl.BlockSpec(a.shape, lambda i: (0, 0))
    wq, wk, wvt = wq.astype(BF16), wk.astype(BF16), wvt.astype(BF16)
    qnw, kvnw = q_norm_w.reshape(1, ql).astype(F32), kv_norm_w.reshape(1, kvl).astype(F32)
    return pl.pallas_call(
        functools.partial(_mla_prep_kernel, nheads=nh, scale=float(qk) ** -0.5 * LOG2E),
        out_shape=(jax.ShapeDtypeStruct((t, nh * 2 * LANE), BF16), jax.ShapeDtypeStruct((t, nh * 2 * LANE), BF16),
                   jax.ShapeDtypeStruct((nh * dv, t), BF16)),
        grid=(t // TP,),
        in_specs=[pl.BlockSpec((TP, ql), lambda i: (i, blk_q)),
                  pl.BlockSpec((TP, kvl), lambda i: (i, blk_kv)),
                  pl.BlockSpec((TP, LANE), lambda i: (i, blk_kr)),
                  full(qnw), full(kvnw), full(wq), full(wk), full(wvt), row(LANE), row(LANE), row(LANE)],
        out_specs=(row(nh * 2 * LANE), row(nh * 2 * LANE), pl.BlockSpec((nh * dv, TP), lambda i: (0, i))),
        compiler_params=_cp(("parallel",), 32),
        name="mla_prep",
    )(p, p, p, qnw, kvnw, wq, wk, wvt, c_tab, s1_tab, s2_tab)


def _gqa_prep_kernel(pg_ref, qnw_ref, knw_ref, c_ref, s_ref, q_ref, k_ref, vt_ref, *, nq, nkv, scale):
    c, s = c_ref[...], s_ref[...]

    def norm_rope(v, w):
        n = _rms(v, w)
        return n * c + pltpu.roll(n, LANE // 2, 1) * s

    for h in range(nq):
        q_ref[:, h * LANE:(h + 1) * LANE] = (norm_rope(pg_ref[:, h * LANE:(h + 1) * LANE], qnw_ref[...]) * scale).astype(BF16)
    for h in range(nkv):
        o = (nq + h) * LANE
        k_ref[:, h * LANE:(h + 1) * LANE] = norm_rope(pg_ref[:, o:o + LANE], knw_ref[...]).astype(BF16)
    o = (nq + nkv) * LANE
    vt_ref[...] = pg_ref[:, o:o + nkv * LANE].T.astype(BF16)


def gqa_prep(p, blk, q_norm_w, k_norm_w, cos, sin):
    t = p.shape[0]
    nq, nkv = GQA_HEADS, GQA_KV_HEADS
    hd = q_norm_w.shape[0]
    assert hd == LANE
    w = (nq + 2 * nkv) * hd
    c_tab = jnp.concatenate([cos, cos], axis=1)
    s_tab = jnp.concatenate([-sin, sin], axis=1)
    row = lambda n: pl.BlockSpec((TP, n), lambda i: (i, 0))
    one = pl.BlockSpec((1, hd), lambda i: (0, 0))
    return pl.pallas_call(
        functools.partial(_gqa_prep_kernel, nq=nq, nkv=nkv, scale=float(hd) ** -0.5 * LOG2E),
        out_shape=(jax.ShapeDtypeStruct((t, nq * hd), BF16), jax.ShapeDtypeStruct((t, nkv * hd), BF16),
                   jax.ShapeDtypeStruct((nkv * hd, t), BF16)),
        grid=(t // TP,),
        in_specs=[pl.BlockSpec((TP, w), lambda i: (i, blk)), one, one, row(hd), row(hd)],
        out_specs=(row(nq * hd), row(nkv * hd), pl.BlockSpec((nkv * hd, TP), lambda i: (0, i))),
        compiler_params=_cp(("parallel",), 32),
        name="gqa_prep",
    )(p, q_norm_w.reshape(1, hd).astype(F32), k_norm_w.reshape(1, hd).astype(F32), c_tab, s_tab)


def _flash_kernel(*refs, rep, dk, dv, tk, n_lat, aliased):
    if aliased:
        refs = refs[1:]
    if n_lat:
        q_ref, kc_ref, vtc_ref, kl_ref, vtl_ref, o_ref, m_scr, l_scr, acc_scr, s_scr = refs
    else:
        q_ref, kc_ref, vtc_ref, o_ref, m_scr, l_scr, acc_scr = refs

    def chunk_softmax(st, vt):
        mc = jnp.max(st, axis=0, keepdims=True)
        pt = jnp.exp2(st - mc)
        return mc, jnp.sum(pt, axis=0, keepdims=True), _dot(vt, pt.astype(BF16))

    def merge(mc, lc, pv):
        m_old = m_scr[...]
        m_new = jnp.maximum(m_old, mc)
        a_old = jnp.exp2(m_old - m_new)
        a_new = jnp.exp2(mc - m_new)
        m_scr[...] = m_new
        l_scr[...] = a_old * l_scr[...] + a_new * lc
        acc_scr[...] = a_old * acc_scr[...] + a_new * pv

    for r in range(rep):
        q = q_ref[:, r * dk:(r + 1) * dk]
        mc, lc, pv = chunk_softmax(_dot_nt(kc_ref[...], q), vtc_ref[...])
        m_scr[...] = mc
        l_scr[...] = lc
        acc_scr[...] = pv
        if n_lat:
            s_scr[0] = _dot_nt(kl_ref[0:tk, :], q)

            def step(j, cur):
                off_next = pl.multiple_of(jnp.minimum(j + 1, n_lat - 1) * tk, tk)
                s_scr[1 - cur] = _dot_nt(kl_ref[pl.ds(off_next, tk), :], q)
                off = pl.multiple_of(j * tk, tk)
                merge(*chunk_softmax(s_scr[cur], vtl_ref[:, pl.ds(off, tk)]))

            def body(i, carry):
                step(2 * i, 0)
                step(2 * i + 1, 1)
                return carry

            lax.fori_loop(0, n_lat // 2, body, 0)
            if n_lat % 2:
                step(n_lat - 1, 0)
        out = acc_scr[...] / l_scr[...]
        o_ref[:, r * dv:(r + 1) * dv] = out.T.astype(o_ref.dtype)


def attention(q, k, vt, nb, l_lat, l_ctx, nl, hkv, rep, dk, dv, need_ctx):
    t = q.shape[0]
    tq = 512 if l_lat % 512 == 0 else TP
    tk = 512 if l_lat % 512 == 0 else TP
    nq = l_lat // tq
    cb = nl // l_ctx
    scratch = lambda n: [pltpu.VMEM((1, n), F32), pltpu.VMEM((1, n), F32), pltpu.VMEM((dv, n), F32)]
    o = pl.pallas_call(
        functools.partial(_flash_kernel, rep=rep, dk=dk, dv=dv, tk=tk, n_lat=l_lat // tk, aliased=False),
        out_shape=jax.ShapeDtypeStruct((t, hkv * rep * dv), BF16),
        grid=(nb, hkv, nq),
        in_specs=[pl.BlockSpec((tq, rep * dk), lambda b, g, i: (b * nq + i, g)),
                  pl.BlockSpec((l_ctx, dk), lambda b, g, i: (cb + b, g)),
                  pl.BlockSpec((dv, l_ctx), lambda b, g, i: (g, cb + b)),
                  pl.BlockSpec((l_lat, dk), lambda b, g, i: (b, g)),
                  pl.BlockSpec((dv, l_lat), lambda b, g, i: (g, b))],
        out_specs=pl.BlockSpec((tq, rep * dv), lambda b, g, i: (b * nq + i, g)),
        scratch_shapes=scratch(tq) + [pltpu.VMEM((2, tk, tq), F32)],
        compiler_params=_cp(("parallel", "parallel", "arbitrary"), 48),
        name="attention_latent",
    )(q, k, vt, k, vt)
    if not need_ctx:
        return o
    return pl.pallas_call(
        functools.partial(_flash_kernel, rep=rep, dk=dk, dv=dv, tk=tk, n_lat=0, aliased=True),
        out_shape=jax.ShapeDtypeStruct((t, hkv * rep * dv), BF16),
        grid=(nb, hkv),
        in_specs=[pl.BlockSpec(memory_space=pl.ANY),
                  pl.BlockSpec((l_ctx, rep * dk), lambda b, g: (cb + b, g)),
                  pl.BlockSpec((l_ctx, dk), lambda b, g: (cb + b, g)),
                  pl.BlockSpec((dv, l_ctx), lambda b, g: (g, cb + b))],
        out_specs=pl.BlockSpec((l_ctx, rep * dv), lambda b, g: (cb + b, g)),
        scratch_shapes=scratch(l_ctx),
        input_output_aliases={0: 0},
        compiler_params=_cp(("parallel", "parallel"), 32),
        name="attention_context",
    )(o, q, k, vt)


def _gelu_tanh(x):
    return 0.5 * x * (1.0 + jnp.tanh(math.sqrt(2.0 / math.pi) * (x + 0.044715 * (x * x * x))))


def _outproj_kernel(x_ref, g_ref, ys5_ref, u_ref, d_ref, gw_ref, gb_ref, yssd_ref, ymla_ref, ygqa_ref, w_ref, o_ref):
    wd = ys5_ref.shape[1]
    gl = _gelu_tanh(ys5_ref[...] + d_ref[...] * u_ref[...])
    s5 = gl * _sigmoid(_dot(gl.astype(BF16), gw_ref[...]) + gb_ref[...])
    acc = _dot(s5.astype(BF16), w_ref[0:wd, :])
    acc += _dot(yssd_ref[...], w_ref[wd:2 * wd, :])
    acc += _dot(ymla_ref[...], w_ref[2 * wd:3 * wd, :])
    acc += _dot(ygqa_ref[...], w_ref[3 * wd:4 * wd, :])
    o_ref[...] = x_ref[...] + g_ref[...] * acc


def out_projection(x, nblk, row_of_block, mod4, ys5, p, u_blk, d_s5, glu_w, glu_b, yssd, ymla, ygqa, w_out):
    t, d = x.shape
    wd = ys5.shape[1]
    row = lambda n: pl.BlockSpec((TM, n), lambda i: (i, 0))
    one = lambda n: pl.BlockSpec((1, n), lambda i: (0, 0))
    return pl.pallas_call(
        _outproj_kernel,
        out_shape=jax.ShapeDtypeStruct((t, d), F32),
        grid=(nblk,),
        in_specs=[row(d),
                  pl.BlockSpec((None, None, 1, d), lambda i: (row_of_block(i), 5, 0, 0)),
                  row(wd),
                  pl.BlockSpec((TM, wd), lambda i: (i, u_blk)),
                  one(wd),
                  pl.BlockSpec((wd, wd), lambda i: (0, 0)),
                  one(wd),
                  row(wd), row(wd), row(wd),
                  pl.BlockSpec((d, d), lambda i: (0, 0))],
        out_specs=row(d),
        compiler_params=_cp(("parallel",), 48),
        name="out_projection",
    )(x, mod4, ys5, p, d_s5.reshape(1, wd).astype(F32), glu_w.astype(BF16), glu_b.reshape(1, wd).astype(F32),
      yssd, ymla, ygqa, w_out)


def _final_norm_kernel(x_ref, w_ref, o_ref):
    o_ref[...] = _rms(x_ref[...], w_ref[...])


def final_norm(x, nrows, w):
    d = x.shape[1]
    return pl.pallas_call(
        _final_norm_kernel,
        out_shape=jax.ShapeDtypeStruct((nrows, d), F32),
        grid=(nrows // TM,),
        in_specs=[pl.BlockSpec((TM, d), lambda i: (i, 0)), pl.BlockSpec((1, d), lambda i: (0, 0))],
        out_specs=pl.BlockSpec((TM, d), lambda i: (i, 0)),
        compiler_params=_cp(("parallel",), 32),
        name="final_norm",
    )(x, w.reshape(1, d).astype(F32))


COL_XBC, COL_GQA, COL_S5, COL_Z, COL_QL, COL_KVL, COL_KR, COL_DT = 0, 1024, 2048, 2560, 3072, 3456, 3584, 3712
N_IN = 3840


def _reorder_w_in(w_in, w_s5, w_ssd, xbc_w, n_dt, ql, kvl, rope, gqa_w):
    d = w_in.shape[0]
    o_ssd = w_s5
    o_mla = o_ssd + w_ssd + xbc_w + n_dt
    o_gqa = o_mla + ql + kvl + rope
    z = lambda n: jnp.zeros((d, n), w_in.dtype)
    parts = [w_in[:, o_ssd + w_ssd:o_ssd + w_ssd + xbc_w],
             w_in[:, o_gqa:o_gqa + gqa_w],
             w_in[:, 0:w_s5],
             w_in[:, o_ssd:o_ssd + w_ssd],
             w_in[:, o_mla:o_mla + ql],
             w_in[:, o_mla + ql:o_mla + ql + kvl],
             w_in[:, o_mla + ql + kvl:o_mla + ql + kvl + rope], z(LANE - rope),
             w_in[:, o_ssd + w_ssd + xbc_w:o_ssd + w_ssd + xbc_w + n_dt], z(LANE - n_dt)]
    out = jnp.concatenate(parts, axis=1)
    assert out.shape[1] == N_IN
    return out.astype(BF16)


def kernel(x, c, ctx, c_ctx, w_ada, b_ada, norm_w, ffn_w1, ffn_w3, ffn_w2, w_in, w_out, s5_lam_re, s5_lam_im, s5_log_dt, s5_b_re, s5_b_im, s5_c_re, s5_c_im, s5_d, s5_glu_w, s5_glu_b, ssd_conv_w, ssd_conv_b, ssd_dt_bias, ssd_a_log, ssd_d, ssd_norm_w, mla_q_norm_w, mla_w_qb, mla_kv_norm_w, mla_w_kvb, gqa_q_norm_w, gqa_k_norm_w, norm_f):
    nb, l_lat, d = x.shape
    l_ctx = ctx.shape[1]
    depth = w_ada.shape[0]
    nl, nc = nb * l_lat, nb * l_ctx
    t = nl + nc
    assert l_ctx == TP and nc == TM and l_lat % TM == 0 and l_lat % GRID_W == 0 and nb < SUBLANE
    assert SUBLANE % nb == 0 and (l_ctx // S5_CHUNK * nb) % SUBLANE == 0 and (l_lat // S5_CHUNK * nb) % SUBLANE == 0
    w_s5 = s5_d.shape[1]
    w_ssd = ssd_norm_w.shape[1]
    xbc_w = ssd_conv_w.shape[2]
    nheads = ssd_a_log.shape[2]
    ql, kvl = mla_q_norm_w.shape[1], mla_kv_norm_w.shape[1]
    gqa_hd = gqa_q_norm_w.shape[1]
    gqa_w = (GQA_HEADS + 2 * GQA_KV_HEADS) * gqa_hd
    assert (w_s5, w_ssd, xbc_w, ql, kvl, gqa_w) == (512, 512, 1024, 384, 128, 1024)

    xs = jnp.concatenate([x.reshape(nl, d), ctx.reshape(nc, d)], axis=0)
    act_in = jnp.concatenate([c, c_ctx[None, :], jnp.zeros((SUBLANE - nb - 1, d), F32)], axis=0)
    mod = ada_modulation(act_in, w_ada, b_ada)
    blocks_per_batch = l_lat // TM
    n_lat_blk = nl // TM
    row_of_block = lambda i: jnp.where(i < n_lat_blk, i // blocks_per_batch, nb)
    cos_m, sin_m = rope_tables(l_lat, MLA_ROPE, nb, l_ctx)
    cos_g, sin_g = rope_tables(l_lat, gqa_hd, nb, l_ctx)

    for l in range(depth):
        need_ctx = l < depth - 1
        mod4 = mod[l].reshape(SUBLANE, N_MOD, 1, d)
        nblk_all = t // TM
        nblk_tail = nblk_all if need_ctx else n_lat_blk
        w1, w3, w2 = ffn_w1[l].astype(BF16), ffn_w3[l].astype(BF16), ffn_w2[l].astype(BF16)
        xs = half_ffn(xs, nblk_all, row_of_block, mod4, 0, norm_w[l, 0], w1[0], w3[0], w2[0])
        w_in_r = _reorder_w_in(w_in[l], w_s5, w_ssd, xbc_w, 2 * nheads, ql, kvl, MLA_ROPE, gqa_w)
        p = in_projection(xs, nblk_all, row_of_block, mod4, norm_w[l, 1], w_in_r)

        tables = s5_tables(s5_lam_re[l], s5_lam_im[l], s5_log_dt[l], s5_b_re[l], s5_b_im[l], s5_c_re[l], s5_c_im[l])
        ys5 = s5_scan(p, nl, nb, l_lat, l_ctx, COL_S5 // 512, tables)

        xbc = ssd_conv(p, COL_XBC // xbc_w, ssd_conv_w[l], ssd_conv_b[l], l_lat, nl)
        dtt = p[:, COL_DT:COL_DT + 2 * nheads].T
        yssd = ssd_scan(xbc, p, COL_DT // LANE, COL_Z // 512, dtt, ssd_dt_bias[l], ssd_a_log[l], ssd_d[l],
                        ssd_norm_w[l], nb, l_lat, l_ctx, nl)

        qm, km, vtm = mla_prep(p, COL_QL // ql, COL_KVL // kvl, COL_KR // LANE, mla_q_norm_w[l], mla_kv_norm_w[l],
                               mla_w_qb[l], mla_w_kvb[l], cos_m, sin_m)
        ymla = attention(qm, km, vtm, nb, l_lat, l_ctx, nl, MLA_HEADS, 1, 2 * LANE, vtm.shape[0] // MLA_HEADS, need_ctx)
        qg, kg, vtg = gqa_prep(p, COL_GQA // gqa_w, gqa_q_norm_w[l], gqa_k_norm_w[l], cos_g, sin_g)
        ygqa = attention(qg, kg, vtg, nb, l_lat, l_ctx, nl, GQA_KV_HEADS, GQA_HEADS // GQA_KV_HEADS, gqa_hd, gqa_hd, need_ctx)

        xs = out_projection(xs, nblk_tail, row_of_block, mod4, ys5, p, COL_S5 // 512, s5_d[l], s5_glu_w[l], s5_glu_b[l],
                            yssd, ymla, ygqa, w_out[l].astype(BF16))
        xs = half_ffn(xs, nblk_tail, row_of_block, mod4, 6, norm_w[l, 2], w1[1], w3[1], w2[1])
    return final_norm(xs, nl, norm_f).reshape(nb, l_lat, d)
```

```python
import functools
import math

import jax
import jax.numpy as jnp
from jax import lax
from jax.experimental import pallas as pl
from jax.experimental.pallas import tpu as pltpu

F32 = jnp.float32
BF16 = jnp.bfloat16
HIGHEST = lax.Precision.HIGHEST

EPS = 1e-6
ROPE_THETA = 10000.0
GRID_W = 64
N_MOD = 9
S5_GROUP = 16
S5_STATE = 64
SSD_HEADDIM = 64
SSD_GROUPS = 2
SSD_STATE = 128
SSD_CHUNK = 128
MLA_HEADS = 4
MLA_NOPE = 128
MLA_ROPE = 64
GQA_HEADS = 4
GQA_KV_HEADS = 2

LANE = 128
SUBLANE = 8
TM = 512
TP = 256
S5_CHUNK = 16
MIB = 1024 * 1024
LOG2E = 1.0 / math.log(2.0)


def _cp(sem, vmem_mib):
    return pltpu.CompilerParams(dimension_semantics=sem, vmem_limit_bytes=vmem_mib * MIB)


def _sigmoid(x):
    return 1.0 / (1.0 + jnp.exp(-x))


def _silu(x):
    return x * _sigmoid(x)


def _rms(x, w):
    return x * lax.rsqrt(jnp.mean(x * x, axis=-1, keepdims=True) + EPS) * w


def _dot(a, b):
    return jnp.dot(a, b, preferred_element_type=F32)


def _dot_nt(a, b):
    return lax.dot_general(a, b, (((1,), (1,)), ((), ())), preferred_element_type=F32)


def _dot_tn(a, b):
    return lax.dot_general(a, b, (((0,), (0,)), ((), ())), preferred_element_type=F32)


def _ada_kernel(a_ref, w_ref, b_ref, o_ref):
    act = _silu(a_ref[...]).astype(BF16)
    o_ref[...] = _dot(act, w_ref[...].astype(BF16)) + b_ref[...]


def ada_modulation(act_in, w_ada, b_ada):
    depth, d, n = w_ada.shape
    tn = 1024
    return pl.pallas_call(
        _ada_kernel,
        out_shape=jax.ShapeDtypeStruct((depth, SUBLANE, n), F32),
        grid=(depth, n // tn),
        in_specs=[pl.BlockSpec((SUBLANE, d), lambda l, j: (0, 0)),
                  pl.BlockSpec((None, d, tn), lambda l, j: (l, 0, j)),
                  pl.BlockSpec((None, 1, tn), lambda l, j: (l, 0, j))],
        out_specs=pl.BlockSpec((None, SUBLANE, tn), lambda l, j: (l, 0, j)),
        compiler_params=_cp(("parallel", "arbitrary"), 40),
        name="ada_modulation",
    )(act_in, w_ada, b_ada.reshape(depth, 1, n))


def _ffn_kernel(x_ref, nw_ref, sh_ref, sc_ref, g_ref, w1_ref, w3_ref, w2_ref, o_ref, h_scr):
    j = pl.program_id(1)

    @pl.when(j == 0)
    def _():
        x = x_ref[...]
        h = _rms(x, nw_ref[...]) * (1.0 + sc_ref[...]) + sh_ref[...]
        h_scr[...] = h.astype(BF16)
        o_ref[...] = jnp.zeros_like(o_ref)

    h = h_scr[...]
    a = _dot(h, w1_ref[...])
    b = _dot(h, w3_ref[...])
    u = (_silu(a) * b).astype(BF16)
    o_ref[...] += _dot(u, w2_ref[...])

    @pl.when(j == pl.num_programs(1) - 1)
    def _():
        o_ref[...] = x_ref[...] + 0.5 * g_ref[...] * o_ref[...]


def half_ffn(x, nblk, row_of_block, mod4, k0, norm_w, w1, w3, w2):
    t, d = x.shape
    f = w1.shape[1]
    tf = 512
    mspec = lambda k: pl.BlockSpec((None, None, 1, d), lambda i, j: (row_of_block(i), k, 0, 0))
    return pl.pallas_call(
        _ffn_kernel,
        out_shape=jax.ShapeDtypeStruct((t, d), F32),
        grid=(nblk, f // tf),
        in_specs=[pl.BlockSpec((TM, d), lambda i, j: (i, 0)),
                  pl.BlockSpec((1, d), lambda i, j: (0, 0)),
                  mspec(k0), mspec(k0 + 1), mspec(k0 + 2),
                  pl.BlockSpec((d, tf), lambda i, j: (0, j)),
                  pl.BlockSpec((d, tf), lambda i, j: (0, j)),
                  pl.BlockSpec((tf, d), lambda i, j: (j, 0))],
        out_specs=pl.BlockSpec((TM, d), lambda i, j: (i, 0)),
        scratch_shapes=[pltpu.VMEM((TM, d), BF16)],
        compiler_params=_cp(("parallel", "arbitrary"), 48),
        name="half_ffn",
    )(x, norm_w.reshape(1, d), mod4, mod4, mod4, w1, w3, w2)


def _inproj_kernel(x_ref, nw_ref, sh_ref, sc_ref, w_ref, o_ref, h_scr):
    @pl.when(pl.program_id(1) == 0)
    def _():
        h = _rms(x_ref[...], nw_ref[...]) * (1.0 + sc_ref[...]) + sh_ref[...]
        h_scr[...] = h.astype(BF16)

    o_ref[...] = _dot(h_scr[...], w_ref[...])


def in_projection(x, nblk, row_of_block, mod4, norm_w, w):
    t, d = x.shape
    n = w.shape[1]
    tn = 1280
    mspec = lambda k: pl.BlockSpec((None, None, 1, d), lambda i, j: (row_of_block(i), k, 0, 0))
    return pl.pallas_call(
        _inproj_kernel,
        out_shape=jax.ShapeDtypeStruct((t, n), F32),
        grid=(nblk, n // tn),
        in_specs=[pl.BlockSpec((TM, d), lambda i, j: (i, 0)),
                  pl.BlockSpec((1, d), lambda i, j: (0, 0)),
                  mspec(3), mspec(4),
                  pl.BlockSpec((d, tn), lambda i, j: (0, j))],
        out_specs=pl.BlockSpec((TM, tn), lambda i, j: (i, j)),
        scratch_shapes=[pltpu.VMEM((TM, d), BF16)],
        compiler_params=_cp(("parallel", "arbitrary"), 40),
        name="in_projection",
    )(x, norm_w.reshape(1, d), mod4, mod4, w)


def s5_tables(lam_re, lam_im, log_dt, b_re, b_im, c_re, c_im):
    n = S5_CHUNK
    hp = functools.partial(jnp.einsum, precision=HIGHEST)
    step = jnp.exp(log_dt.astype(F32))[..., None]
    lr, li = lam_re.astype(F32), lam_im.astype(F32)
    jj = jnp.arange(n + 1, dtype=F32)
    mag = jnp.exp(lr[..., None] * step[..., None] * jj)
    ang = li[..., None] * step[..., None] * jj
    pr, pi = mag * jnp.cos(ang), mag * jnp.sin(ang)
    ar, ai = pr[..., 1], pi[..., 1]
    den = lr * lr + li * li
    qr = ((ar - 1.0) * lr + ai * li) / den
    qi = (ai * lr - (ar - 1.0) * li) / den
    br, bi = b_re.astype(F32), b_im.astype(F32)
    bbr = qr[..., None] * br - qi[..., None] * bi
    bbi = qr[..., None] * bi + qi[..., None] * br
    cr, ci = c_re.astype(F32), c_im.astype(F32)
    k = (hp('dgop,dgpj,dgpi->dgjoi', cr, pr[..., :n], bbr) - hp('dgop,dgpj,dgpi->dgjoi', cr, pi[..., :n], bbi)
         - hp('dgop,dgpj,dgpi->dgjoi', ci, pr[..., :n], bbi) - hp('dgop,dgpj,dgpi->dgjoi', ci, pi[..., :n], bbr))
    g = k.shape[1]
    h = k.shape[-1]
    s_idx = jnp.arange(n)[:, None]
    r_idx = jnp.arange(n)[None, :]
    lag = r_idx - s_idx
    kf = jnp.where((lag >= 0)[None, :, :, None, None], k[0][:, jnp.clip(lag, 0, n - 1)], 0.0)
    kb = jnp.where((lag <= 0)[None, :, :, None, None], k[1][:, jnp.clip(-lag, 0, n - 1)], 0.0)
    toep = jnp.transpose(kf + kb, (0, 1, 4, 2, 3)).reshape(g, n * h, n * h)

    def bend(d, pw_r, pw_i):
        re = hp('gps,gpi->gsip', pw_r, bbr[d]) - hp('gps,gpi->gsip', pw_i, bbi[d])
        im = hp('gps,gpi->gsip', pw_r, bbi[d]) + hp('gps,gpi->gsip', pw_i, bbr[d])
        return re.reshape(g, n * h, -1), im.reshape(g, n * h, -1)

    bf_r, bf_i = bend(0, pr[0][..., n - 1::-1][..., :n], pi[0][..., n - 1::-1][..., :n])
    bb_r, bb_i = bend(1, pr[1][..., :n], pi[1][..., :n])

    def coff(d, pw_r, pw_i):
        re = hp('gop,gpr->gpro', cr[d], pw_r) - hp('gop,gpr->gpro', ci[d], pw_i)
        im = -(hp('gop,gpr->gpro', cr[d], pw_i) + hp('gop,gpr->gpro', ci[d], pw_r))
        return re.reshape(g, -1, n * h), im.reshape(g, -1, n * h)

    cf_r, cf_i = coff(0, pr[0][..., 1:], pi[0][..., 1:])
    cb_r, cb_i = coff(1, pr[1][..., n:0:-1], pi[1][..., n:0:-1])

    q = g // 2
    w = n * h
    p = pr.shape[2]

    def pair_diag(m):
        m = m.reshape(q, 2, m.shape[1], m.shape[2])
        z = jnp.zeros_like(m[:, 0])
        return jnp.concatenate([jnp.concatenate([m[:, 0], z], axis=2), jnp.concatenate([z, m[:, 1]], axis=2)], axis=1)

    def step_major(m, axis):
        shp = m.shape
        m = m.reshape(shp[:axis] + (2, n, h) + shp[axis + 1:])
        return jnp.swapaxes(m, axis, axis + 1).reshape(shp)

    toep_p = step_major(step_major(pair_diag(toep), 1), 2)
    bend_p = jnp.concatenate([pair_diag(bf_r), pair_diag(bf_i), pair_diag(bb_r), pair_diag(bb_i)], axis=2)
    bend_p = step_major(bend_p, 1)
    coff_p = jnp.concatenate([pair_diag(cf_r), pair_diag(cf_i), pair_diag(cb_r), pair_diag(cb_i)], axis=1)
    coff_p = step_major(coff_p, 2)
    a16 = jnp.stack([pr[0][..., n], pi[0][..., n], pr[1][..., n], pi[1][..., n]], axis=1)
    a16 = a16.reshape(q, 2, 4, p).transpose(0, 2, 1, 3).reshape(q, 4, 2 * p)
    a16 = jnp.concatenate([a16, jnp.zeros((q, 4, 2 * p), F32)], axis=1)
    return toep_p.astype(BF16), bend_p.astype(BF16), coff_p.astype(BF16), a16


def _s5_local_kernel(u_ref, toep_ref, bend_ref, y_ref, h_ref):
    u = u_ref[...]
    y_ref[...] = _dot(u, toep_ref[...])
    h_ref[...] = _dot(u, bend_ref[...])


def _s5_carry_kernel(a_ref, hl_ref, hs_ref, *, nb, n_ctx_tiles, n_lat_tiles):
    a = a_ref[...]
    afr, afi, abr, abi = a[0:1], a[1:2], a[2:3], a[3:4]
    spt = SUBLANE // nb

    def tile_pass(tile_f, tile_b, carry):
        fr, fi, br, bi = carry
        rf = pl.multiple_of(tile_f * SUBLANE, SUBLANE)
        rb = pl.multiple_of(tile_b * SUBLANE, SUBLANE)
        lf = hl_ref[pl.ds(rf, SUBLANE), 0:2 * LANE]
        lb = hl_ref[pl.ds(rb, SUBLANE), 2 * LANE:4 * LANE]
        of_r, of_i, ob_r, ob_i = [], [], [], []
        for k in range(spt):
            of_r.append(fr)
            of_i.append(fi)
            lo = k * nb
            nr = afr * fr - afi * fi + lf[lo:lo + nb, 0:LANE]
            ni = afr * fi + afi * fr + lf[lo:lo + nb, LANE:2 * LANE]
            fr, fi = nr, ni
        for k in range(spt - 1, -1, -1):
            ob_r.append(br)
            ob_i.append(bi)
            lo = k * nb
            nr = abr * br - abi * bi + lb[lo:lo + nb, 0:LANE]
            ni = abr * bi + abi * br + lb[lo:lo + nb, LANE:2 * LANE]
            br, bi = nr, ni
        hs_ref[pl.ds(rf, SUBLANE), 0:LANE] = jnp.concatenate(of_r, axis=0)
        hs_ref[pl.ds(rf, SUBLANE), LANE:2 * LANE] = jnp.concatenate(of_i, axis=0)
        hs_ref[pl.ds(rb, SUBLANE), 2 * LANE:3 * LANE] = jnp.concatenate(ob_r[::-1], axis=0)
        hs_ref[pl.ds(rb, SUBLANE), 3 * LANE:4 * LANE] = jnp.concatenate(ob_i[::-1], axis=0)
        return fr, fi, br, bi

    z = jnp.zeros((nb, LANE), F32)
    carry = (z, z, z, z)
    carry = lax.fori_loop(0, n_ctx_tiles, lambda t, c: tile_pass(t, n_ctx_tiles - 1 - t, c), carry)
    lax.fori_loop(0, n_lat_tiles,
                  lambda t, c: tile_pass(n_ctx_tiles + t, n_ctx_tiles + n_lat_tiles - 1 - t, c), carry)


def _s5_out_kernel(yi_ref, hs_ref, coff_ref, y_ref):
    y_ref[...] = yi_ref[...] + _dot(hs_ref[...].astype(BF16), coff_ref[...])


def s5_scan(p, nl, nb, l_lat, l_ctx, col_blk, tables):
    toep, bend, coff, a16 = tables
    q = toep.shape[0]
    w = 2 * S5_CHUNK * S5_GROUP
    t = p.shape[0]
    u = p[:, col_blk * w:(col_blk + 1) * w].astype(BF16)
    c_lat, c_ctx = l_lat // S5_CHUNK, l_ctx // S5_CHUNK
    wp = 2 * S5_GROUP

    def to_chunks(v, nchunk):
        v = v.reshape(nb, nchunk, S5_CHUNK, q, wp)
        return jnp.transpose(v, (3, 1, 0, 2, 4)).reshape(q, nchunk * nb, w)

    def from_chunks(v, nchunk):
        v = v.reshape(q, nchunk, nb, S5_CHUNK, wp)
        return jnp.transpose(v, (2, 1, 3, 0, 4)).reshape(nb * nchunk * S5_CHUNK, w)

    uc = jnp.concatenate([to_chunks(u[nl:], c_ctx), to_chunks(u[:nl], c_lat)], axis=1)
    nrow = uc.shape[1]
    yi, hl = pl.pallas_call(
        _s5_local_kernel,
        out_shape=(jax.ShapeDtypeStruct((q, nrow, w), F32), jax.ShapeDtypeStruct((nrow, q * w), F32)),
        grid=(q,),
        in_specs=[pl.BlockSpec((None, nrow, w), lambda g: (g, 0, 0)),
                  pl.BlockSpec((None, w, w), lambda g: (g, 0, 0)),
                  pl.BlockSpec((None, w, w), lambda g: (g, 0, 0))],
        out_specs=(pl.BlockSpec((None, nrow, w), lambda g: (g, 0, 0)),
                   pl.BlockSpec((nrow, w), lambda g: (0, g))),
        compiler_params=_cp(("parallel",), 40),
        name="s5_local",
    )(uc, toep, bend)
    hs = pl.pallas_call(
        functools.partial(_s5_carry_kernel, nb=nb, n_ctx_tiles=c_ctx * nb // SUBLANE,
                          n_lat_tiles=c_lat * nb // SUBLANE),
        out_shape=jax.ShapeDtypeStruct((nrow, q * w), F32),
        grid=(q,),
        in_specs=[pl.BlockSpec((None, SUBLANE, LANE), lambda g: (g, 0, 0)),
                  pl.BlockSpec((nrow, w), lambda g: (0, g))],
        out_specs=pl.BlockSpec((nrow, w), lambda g: (0, g)),
        compiler_params=_cp(("parallel",), 40),
        name="s5_carry",
    )(a16, hl)
    y = pl.pallas_call(
        _s5_out_kernel,
        out_shape=jax.ShapeDtypeStruct((q, nrow, w), F32),
        grid=(q,),
        in_specs=[pl.BlockSpec((None, nrow, w), lambda g: (g, 0, 0)),
                  pl.BlockSpec((nrow, w), lambda g: (0, g)),
                  pl.BlockSpec((None, w, w), lambda g: (g, 0, 0))],
        out_specs=pl.BlockSpec((None, nrow, w), lambda g: (g, 0, 0)),
        compiler_params=_cp(("parallel",), 40),
        name="s5_out",
    )(yi, hs, coff)
    n_ctx_rows = c_ctx * nb
    return jnp.concatenate([from_chunks(y[:, n_ctx_rows:], c_lat), from_chunks(y[:, :n_ctx_rows], c_ctx)], axis=0)


def _ssd_conv_kernel(x_ref, prev_ref, next_ref, w_ref, b_ref, o_ref, pad_scr, *, blocks_per_seq, n_lat_blocks):
    i = pl.program_id(0)
    is_lat = i < n_lat_blocks
    first = jnp.logical_or(jnp.logical_not(is_lat), i % blocks_per_seq == 0)
    last = jnp.logical_or(jnp.logical_not(is_lat), (i + 1) % blocks_per_seq == 0)
    tp = x_ref.shape[0]
    pad_scr[0:SUBLANE, :] = jnp.where(first, 0.0, prev_ref[...])
    pad_scr[SUBLANE:SUBLANE + tp, :] = x_ref[...]
    pad_scr[SUBLANE + tp:2 * SUBLANE + tp, :] = jnp.where(last, 0.0, next_ref[...])
    w = w_ref[...]
    kk = 5
    acc = b_ref[...] + w[0:1] * pad_scr[SUBLANE - kk // 2:SUBLANE - kk // 2 + tp, :]
    for k in range(1, kk):
        off = SUBLANE - kk // 2 + k
        acc = acc + w[k:k + 1] * pad_scr[off:off + tp, :]
    o_ref[...] = _silu(acc)


def ssd_conv(p, col_blk, conv_w, conv_b, l_lat, nl):
    t = p.shape[0]
    c = conv_w.shape[1]
    assert conv_w.shape[0] == 5
    r = TP // SUBLANE
    wpad = jnp.concatenate([conv_w.astype(F32), jnp.zeros((SUBLANE - conv_w.shape[0], c), F32)], axis=0)
    nsub = t // SUBLANE
    return pl.pallas_call(
        functools.partial(_ssd_conv_kernel, blocks_per_seq=l_lat // TP, n_lat_blocks=nl // TP),
        out_shape=jax.ShapeDtypeStruct((t, c), F32),
        grid=(t // TP,),
        in_specs=[pl.BlockSpec((TP, c), lambda i: (i, col_blk)),
                  pl.BlockSpec((SUBLANE, c), lambda i: (jnp.maximum(i * r - 1, 0), col_blk)),
                  pl.BlockSpec((SUBLANE, c), lambda i: (jnp.minimum((i + 1) * r, nsub - 1), col_blk)),
                  pl.BlockSpec((SUBLANE, c), lambda i: (0, 0)),
                  pl.BlockSpec((1, c), lambda i: (0, 0))],
        out_specs=pl.BlockSpec((TP, c), lambda i: (i, 0)),
        scratch_shapes=[pltpu.VMEM((TP + 2 * SUBLANE, c), F32)],
        compiler_params=_cp(("parallel",), 32),
        name="ssd_conv",
    )(p, p, p, wpad, conv_b.reshape(1, c).astype(F32))


def _softplus(x):
    return jnp.maximum(x, 0.0) + jnp.log(1.0 + jnp.exp(-jnp.abs(x)))


def _split_bf16(x, terms):
    out = []
    for _ in range(terms):
        piece = x.astype(BF16)
        out.append(piece)
        x = x - piece.astype(F32)
    return out


def _ssd_chunk(xbc_ref, dt_ref, dtt_ref, bias, biast, a_row, a_col, e, st_ref, *, reverse, lane0, nheads, hd, ns,
               ngroups):
    tc = xbc_ref.shape[0]
    wx = nheads * hd
    hpg = nheads // ngroups
    x = xbc_ref[:, 0:wx]
    rows = lax.broadcasted_iota(jnp.int32, (tc, tc), 0)
    cols = lax.broadcasted_iota(jnp.int32, (tc, tc), 1)
    keep = (cols >= rows) if reverse else (cols <= rows)
    tri = keep.astype(BF16)
    tri_t = ((rows >= cols) if reverse else (rows <= cols)).astype(BF16)
    dt = _softplus(dt_ref[...] + bias)
    acs = sum(_dot(tri, piece) for piece in _split_bf16(dt * a_row, 3))
    dtt = _softplus(dtt_ref[...] + biast)
    acst = sum(_dot(piece, tri_t) for piece in _split_bf16(dtt * a_col, 3))
    edge = tc - 1 if not reverse else 0
    tot = acs[edge:edge + 1, :]
    fac = jnp.concatenate([dt, jnp.exp(acs), jnp.exp(tot - acs)], axis=0)
    spread = sum(_dot(piece, e) for piece in _split_bf16(fac, 2))
    dtx, eacs, edec = spread[0:tc], spread[tc:2 * tc], spread[2 * tc:3 * tc]
    chunk_decay = eacs[edge:edge + 1, :]
    xd = x * dtx
    xdb = xd.astype(BF16)
    xdd = (xd * edec).astype(BF16)
    ys = []
    for g in range(ngroups):
        bg = xbc_ref[:, wx + g * ns:wx + (g + 1) * ns].astype(BF16)
        cg = xbc_ref[:, wx + (ngroups + g) * ns:wx + (ngroups + g + 1) * ns].astype(BF16)
        cb = _dot_nt(cg, bg)
        lo, hi = g * hpg * hd, (g + 1) * hpg * hd
        st = st_ref[:, lo:hi]
        y_off = _dot(cg, st.astype(BF16)) * eacs[:, lo:hi]
        st_ref[:, lo:hi] = st * chunk_decay[:, lo:hi] + _dot_tn(bg, xdd[:, lo:hi])
        yd = []
        for hh in range(hpg):
            h = g * hpg + hh
            col = acs[:, lane0 + h:lane0 + h + 1]
            row = acst[lane0 + h:lane0 + h + 1, :]
            lm = jnp.exp(jnp.where(keep, col - row, -jnp.inf))
            yd.append(_dot((cb * lm).astype(BF16), xdb[:, h * hd:(h + 1) * hd]))
        ys.append(jnp.concatenate(yd, axis=1) + y_off)
    return jnp.concatenate(ys, axis=1)


def _ssd_scan_kernel(xf_ref, dtf_ref, dttf_ref, xb_ref, dtb_ref, dttb_ref, bias_ref, biast_ref, a_ref, at_ref,
                     ef_ref, eb_ref, yf_ref, yb_ref, st_scr, *, nheads, hd, ns, ngroups):
    @pl.when(pl.program_id(1) == 0)
    def _():
        st_scr[...] = jnp.zeros_like(st_scr)

    kw = dict(nheads=nheads, hd=hd, ns=ns, ngroups=ngroups)
    bias, biast, a_row, a_col = bias_ref[...], biast_ref[...], a_ref[...], at_ref[...]
    yf_ref[...] = _ssd_chunk(xf_ref, dtf_ref, dttf_ref, bias, biast, a_row, a_col, ef_ref[...], st_scr.at[0],
                             reverse=False, lane0=0, **kw)
    yb_ref[...] = _ssd_chunk(xb_ref, dtb_ref, dttb_ref, bias, biast, a_row, a_col, eb_ref[...], st_scr.at[1],
                             reverse=True, lane0=nheads, **kw)


def ssd_scan(xbc, p, dt_blk, dtt, dt_bias, a_log, nb, l_lat, l_ctx, nl):
    t = xbc.shape[0]
    nheads = a_log.shape[1]
    hd = SSD_HEADDIM
    wx = nheads * hd
    tc = SSD_CHUNK
    nch_ctx, nch_lat = l_ctx // tc, l_lat // tc
    nch = nch_ctx + nch_lat
    a = -jnp.exp(a_log.astype(F32))
    a_row = jnp.zeros((1, LANE), F32).at[0, :2 * nheads].set(a.reshape(-1))
    bias_row = jnp.zeros((1, LANE), F32).at[0, :2 * nheads].set(dt_bias.astype(F32).reshape(-1))
    a_col = a.reshape(2 * nheads, 1)
    bias_col = dt_bias.astype(F32).reshape(2 * nheads, 1)

    def expand(direction):
        hidx = jnp.arange(LANE)[:, None] - direction * nheads
        return (hidx == (jnp.arange(wx)[None, :] // hd)).astype(BF16)

    def chunk_block(reverse):
        def f(b, k):
            is_ctx = k < nch_ctx
            ctx_c = (nch_ctx - 1 - k) if reverse else k
            lat_c = (nch_lat - 1 - (k - nch_ctx)) if reverse else (k - nch_ctx)
            return jnp.where(is_ctx, (nl + b * l_ctx) // tc + ctx_c, b * nch_lat + lat_c)
        return f

    blk_f, blk_b = chunk_block(False), chunk_block(True)
    chunk_specs = lambda blk: [pl.BlockSpec((tc, xbc.shape[1]), lambda b, k: (blk(b, k), 0)),
                               pl.BlockSpec((tc, LANE), lambda b, k: (blk(b, k), dt_blk)),
                               pl.BlockSpec((2 * nheads, tc), lambda b, k: (0, blk(b, k)))]
    const = lambda shape: pl.BlockSpec(shape, lambda b, k: (0, 0))
    return pl.pallas_call(
        functools.partial(_ssd_scan_kernel, nheads=nheads, hd=hd, ns=SSD_STATE, ngroups=SSD_GROUPS),
        out_shape=(jax.ShapeDtypeStruct((t, wx), F32), jax.ShapeDtypeStruct((t, wx), F32)),
        grid=(nb, nch),
        in_specs=chunk_specs(blk_f) + chunk_specs(blk_b) + [const((1, LANE)), const((2 * nheads, 1)), const((1, LANE)),
                                                            const((2 * nheads, 1)), const((LANE, wx)), const((LANE, wx))],
        out_specs=(pl.BlockSpec((tc, wx), lambda b, k: (blk_f(b, k), 0)),
                   pl.BlockSpec((tc, wx), lambda b, k: (blk_b(b, k), 0))),
        scratch_shapes=[pltpu.VMEM((2, SSD_STATE, wx), F32)],
        compiler_params=_cp(("parallel", "arbitrary"), 32),
        name="ssd_scan",
    )(xbc, p, dtt, xbc, p, dtt, bias_row, bias_col, a_row, a_col, expand(0), expand(1))


def rope_tables(l_lat, rot_dim, nb, l_ctx):
    rows = l_lat // GRID_W
    row = jnp.repeat(jnp.arange(rows, dtype=F32), GRID_W)
    col = jnp.tile(jnp.arange(GRID_W, dtype=F32), rows)
    n_freq = rot_dim // 4
    inv_freq = ROPE_THETA ** (-jnp.arange(n_freq, dtype=F32) / n_freq)
    ang = jnp.concatenate([row[:, None] * inv_freq, col[:, None] * inv_freq], axis=-1)
    cos = jnp.concatenate([jnp.tile(jnp.cos(ang), (nb, 1)), jnp.ones((nb * l_ctx, rot_dim // 2), F32)], axis=0)
    sin = jnp.concatenate([jnp.tile(jnp.sin(ang), (nb, 1)), jnp.zeros((nb * l_ctx, rot_dim // 2), F32)], axis=0)
    return cos, sin


def _mla_prep_kernel(ql_ref, kvl_ref, kr_ref, qnw_ref, kvnw_ref, wq_ref, wk_ref, wvt_ref, c_ref, s1_ref, s2_ref,
                     q_ref, k_ref, vt_ref, *, nheads, scale):
    qn = _rms(ql_ref[...], qnw_ref[...]).astype(BF16)
    q = _dot(qn, wq_ref[...])
    ckv = _rms(kvl_ref[...], kvnw_ref[...]).astype(BF16)
    kn = _dot(ckv, wk_ref[...])
    vt_ref[...] = _dot_nt(wvt_ref[...], ckv).astype(BF16)
    c, s1, s2 = c_ref[...], s1_ref[...], s2_ref[...]

    def rope(v):
        return v * c + pltpu.roll(v, 3 * LANE // 4, 1) * s1 + pltpu.roll(v, LANE // 4, 1) * s2

    kr = rope(kr_ref[...]).astype(BF16)
    for h in range(nheads):
        b0 = 2 * LANE * h
        q_ref[:, b0:b0 + LANE] = (q[:, b0:b0 + LANE] * scale).astype(BF16)
        q_ref[:, b0 + LANE:b0 + 2 * LANE] = (rope(q[:, b0 + LANE:b0 + 2 * LANE]) * scale).astype(BF16)
        k_ref[:, b0:b0 + LANE] = kn[:, LANE * h:LANE * (h + 1)].astype(BF16)
        k_ref[:, b0 + LANE:b0 + 2 * LANE] = kr


def mla_prep(p, blk_q, blk_kv, blk_kr, q_norm_w, kv_norm_w, w_qb, w_kvb, cos, sin):
    t = p.shape[0]
    nh = MLA_HEADS
    qk = MLA_NOPE + MLA_ROPE
    ql, kvl = q_norm_w.shape[0], kv_norm_w.shape[0]
    assert MLA_NOPE == LANE and MLA_ROPE == LANE // 2 and kvl == LANE
    dv = w_kvb.shape[1] // nh - MLA_NOPE
    wq = w_qb.reshape(ql, nh, qk)
    wq = jnp.concatenate([wq, jnp.zeros((ql, nh, 2 * LANE - qk), w_qb.dtype)], axis=-1).reshape(ql, nh * 2 * LANE)
    wkv = w_kvb.reshape(kvl, nh, MLA_NOPE + dv)
    wk = wkv[:, :, :MLA_NOPE].reshape(kvl, nh * MLA_NOPE)
    wvt = wkv[:, :, MLA_NOPE:].reshape(kvl, nh * dv).T
    half = MLA_ROPE // 2
    zeros = jnp.zeros((t, half), F32)
    c_tab = jnp.concatenate([cos, cos, jnp.ones((t, 2 * half), F32)], axis=1)
    s1_tab = jnp.concatenate([-sin, zeros, zeros, zeros], axis=1)
    s2_tab = jnp.concatenate([zeros, sin, zeros, zeros], axis=1)
    row = lambda w: pl.BlockSpec((TP, w), lambda i: (i, 0))
    full = lambda a: pl.BlockSpec(a.shape, lambda i: (0, 0))
    wq, wk, wvt = wq.astype(BF16), wk.astype(BF16), wvt.astype(BF16)
    qnw, kvnw = q_norm_w.reshape(1, ql).astype(F32), kv_norm_w.reshape(1, kvl).astype(F32)
    return pl.pallas_call(
        functools.partial(_mla_prep_kernel, nheads=nh, scale=float(qk) ** -0.5 * LOG2E),
        out_shape=(jax.ShapeDtypeStruct((t, nh * 2 * LANE), BF16), jax.ShapeDtypeStruct((t, nh * 2 * LANE), BF16),
                   jax.ShapeDtypeStruct((nh * dv, t), BF16)),
        grid=(t // TP,),
        in_specs=[pl.BlockSpec((TP, ql), lambda i: (i, blk_q)),
                  pl.BlockSpec((TP, kvl), lambda i: (i, blk_kv)),
                  pl.BlockSpec((TP, LANE), lambda i: (i, blk_kr)),
                  full(qnw), full(kvnw), full(wq), full(wk), full(wvt), row(LANE), row(LANE), row(LANE)],
        out_specs=(row(nh * 2 * LANE), row(nh * 2 * LANE), pl.BlockSpec((nh * dv, TP), lambda i: (0, i))),
        compiler_params=_cp(("parallel",), 32),
        name="mla_prep",
    )(p, p, p, qnw, kvnw, wq, wk, wvt, c_tab, s1_tab, s2_tab)


def _gqa_prep_kernel(pg_ref, qnw_ref, knw_ref, c_ref, s_ref, q_ref, k_ref, vt_ref, *, nq, nkv, scale):
    c, s = c_ref[...], s_ref[...]

    def norm_rope(v, w):
        n = _rms(v, w)
        return n * c + pltpu.roll(n, LANE // 2, 1) * s

    for h in range(nq):
        q_ref[:, h * LANE:(h + 1) * LANE] = (norm_rope(pg_ref[:, h * LANE:(h + 1) * LANE], qnw_ref[...]) * scale).astype(BF16)
    for h in range(nkv):
        o = (nq + h) * LANE
        k_ref[:, h * LANE:(h + 1) * LANE] = norm_rope(pg_ref[:, o:o + LANE], knw_ref[...]).astype(BF16)
    o = (nq + nkv) * LANE
    vt_ref[...] = pg_ref[:, o:o + nkv * LANE].T.astype(BF16)


def gqa_prep(p, blk, q_norm_w, k_norm_w, cos, sin):
    t = p.shape[0]
    nq, nkv = GQA_HEADS, GQA_KV_HEADS
    hd = q_norm_w.shape[0]
    assert hd == LANE
    w = (nq + 2 * nkv) * hd
    c_tab = jnp.concatenate([cos, cos], axis=1)
    s_tab = jnp.concatenate([-sin, sin], axis=1)
    row = lambda n: pl.BlockSpec((TP, n), lambda i: (i, 0))
    one = pl.BlockSpec((1, hd), lambda i: (0, 0))
    return pl.pallas_call(
        functools.partial(_gqa_prep_kernel, nq=nq, nkv=nkv, scale=float(hd) ** -0.5 * LOG2E),
        out_shape=(jax.ShapeDtypeStruct((t, nq * hd), BF16), jax.ShapeDtypeStruct((t, nkv * hd), BF16),
                   jax.ShapeDtypeStruct((nkv * hd, t), BF16)),
        grid=(t // TP,),
        in_specs=[pl.BlockSpec((TP, w), lambda i: (i, blk)), one, one, row(hd), row(hd)],
        out_specs=(row(nq * hd), row(nkv * hd), pl.BlockSpec((nkv * hd, TP), lambda i: (0, i))),
        compiler_params=_cp(("parallel",), 32),
        name="gqa_prep",
    )(p, q_norm_w.reshape(1, hd).astype(F32), k_norm_w.reshape(1, hd).astype(F32), c_tab, s_tab)


def _flash_kernel(*refs, rep, rows, dk, dv, tk, n_lat, unroll, aliased):
    if aliased:
        refs = refs[1:]
    if n_lat:
        q_ref, kc_ref, vtc_ref, kl_ref, vtl_ref, o_ref, m_scr, l_scr, acc_scr, s_scr = refs
    else:
        q_ref, kc_ref, vtc_ref, o_ref, m_scr, l_scr, acc_scr = refs
    tq = q_ref.shape[0] // rows
    streams = [(h, r) for h in range(rows) for r in range(rep)]
    qs = [q_ref[h * tq:(h + 1) * tq, r * dk:(r + 1) * dk] for h, r in streams]

    def chunk_softmax(st, vt):
        mc = jnp.max(st, axis=0, keepdims=True)
        pt = jnp.exp2(st - mc)
        return mc, jnp.sum(pt, axis=0, keepdims=True), _dot(vt, pt.astype(BF16))

    def merge(s, mc, lc, pv):
        m_old = m_scr[s]
        m_new = jnp.maximum(m_old, mc)
        a_old = jnp.exp2(m_old - m_new)
        a_new = jnp.exp2(mc - m_new)
        m_scr[s] = m_new
        l_scr[s] = a_old * l_scr[s] + a_new * lc
        acc_scr[s] = a_old * acc_scr[s] + a_new * pv

    for s, q in enumerate(qs):
        mc, lc, pv = chunk_softmax(_dot_nt(kc_ref[...], q), vtc_ref[...])
        m_scr[s] = mc
        l_scr[s] = lc
        acc_scr[s] = pv
    if n_lat:
        for s, q in enumerate(qs):
            s_scr[s, 0] = _dot_nt(kl_ref[0:tk, :], q)

        def step(j, cur):
            off_next = pl.multiple_of(jnp.minimum(j + 1, n_lat - 1) * tk, tk)
            off = pl.multiple_of(j * tk, tk)
            for s, q in enumerate(qs):
                s_scr[s, 1 - cur] = _dot_nt(kl_ref[pl.ds(off_next, tk), :], q)
                merge(s, *chunk_softmax(s_scr[s, cur], vtl_ref[:, pl.ds(off, tk)]))

        def body(i, carry):
            for u in range(unroll):
                step(unroll * i + u, u % 2)
            return carry

        lax.fori_loop(0, n_lat // unroll, body, 0)
    for s, (h, r) in enumerate(streams):
        out = acc_scr[s] / l_scr[s]
        o_ref[h * tq:(h + 1) * tq, r * dv:(r + 1) * dv] = out.T.astype(o_ref.dtype)


def attention(q, k, vt, nb, l_lat, l_ctx, nl, hkv, rep, dk, dv, need_ctx):
    t = q.shape[0]
    tq = 512 if l_lat % 512 == 0 else TP
    tk = 1024 if l_lat % 2048 == 0 else (512 if l_lat % 512 == 0 else TP)
    rows = 2 // rep if l_lat % (tq * 2 // rep) == 0 else 1
    n_lat = l_lat // tk
    unroll = 2 if n_lat % 2 == 0 else 1
    nq = l_lat // (tq * rows)
    cb = nl // l_ctx
    scratch = lambda ns, n: [pltpu.VMEM((ns, 1, n), F32), pltpu.VMEM((ns, 1, n), F32), pltpu.VMEM((ns, dv, n), F32)]
    o = pl.pallas_call(
        functools.partial(_flash_kernel, rep=rep, rows=rows, dk=dk, dv=dv, tk=tk, n_lat=n_lat, unroll=unroll,
                          aliased=False),
        out_shape=jax.ShapeDtypeStruct((t, hkv * rep * dv), BF16),
        grid=(nb, hkv, nq),
        in_specs=[pl.BlockSpec((tq * rows, rep * dk), lambda b, g, i: (b * nq + i, g)),
                  pl.BlockSpec((l_ctx, dk), lambda b, g, i: (cb + b, g)),
                  pl.BlockSpec((dv, l_ctx), lambda b, g, i: (g, cb + b)),
                  pl.BlockSpec((l_lat, dk), lambda b, g, i: (b, g)),
                  pl.BlockSpec((dv, l_lat), lambda b, g, i: (g, b))],
        out_specs=pl.BlockSpec((tq * rows, rep * dv), lambda b, g, i: (b * nq + i, g)),
        scratch_shapes=scratch(rows * rep, tq) + [pltpu.VMEM((rows * rep, 2, tk, tq), F32)],
        compiler_params=_cp(("parallel", "parallel", "arbitrary"), 48),
        name="attention_latent",
    )(q, k, vt, k, vt)
    if not need_ctx:
        return o
    return pl.pallas_call(
        functools.partial(_flash_kernel, rep=rep, rows=1, dk=dk, dv=dv, tk=tk, n_lat=0, unroll=1, aliased=True),
        out_shape=jax.ShapeDtypeStruct((t, hkv * rep * dv), BF16),
        grid=(nb, hkv),
        in_specs=[pl.BlockSpec(memory_space=pl.ANY),
                  pl.BlockSpec((l_ctx, rep * dk), lambda b, g: (cb + b, g)),
                  pl.BlockSpec((l_ctx, dk), lambda b, g: (cb + b, g)),
                  pl.BlockSpec((dv, l_ctx), lambda b, g: (g, cb + b))],
        out_specs=pl.BlockSpec((l_ctx, rep * dv), lambda b, g: (cb + b, g)),
        scratch_shapes=scratch(rep, l_ctx),
        input_output_aliases={0: 0},
        compiler_params=_cp(("parallel", "parallel"), 32),
        name="attention_context",
    )(o, q, k, vt)


def _gelu_tanh(x):
    return 0.5 * x * (1.0 + jnp.tanh(math.sqrt(2.0 / math.pi) * (x + 0.044715 * (x * x * x))))


def _outproj_kernel(x_ref, g_ref, ys5_ref, u_ref, d_ref, gw_ref, gb_ref, yf_ref, yb_ref, xs_ref, z_ref, dssd_ref,
                    nw_ref, ymla_ref, ygqa_ref, w_ref, o_ref):
    wd = ys5_ref.shape[1]
    gl = _gelu_tanh(ys5_ref[...] + d_ref[...] * u_ref[...])
    s5 = gl * _sigmoid(_dot(gl.astype(BF16), gw_ref[...]) + gb_ref[...])
    ssd = _rms((yf_ref[...] + yb_ref[...] + dssd_ref[...] * xs_ref[...]) * _silu(z_ref[...]), nw_ref[...])
    acc = _dot(s5.astype(BF16), w_ref[0:wd, :])
    acc += _dot(ssd.astype(BF16), w_ref[wd:2 * wd, :])
    acc += _dot(ymla_ref[...], w_ref[2 * wd:3 * wd, :])
    acc += _dot(ygqa_ref[...], w_ref[3 * wd:4 * wd, :])
    o_ref[...] = x_ref[...] + g_ref[...] * acc


def out_projection(x, nblk, row_of_block, mod4, ys5, p, u_blk, z_blk, d_s5, glu_w, glu_b, yf, yb, xbc, d_ssd, nw_ssd,
                   ymla, ygqa, w_out):
    t, d = x.shape
    wd = ys5.shape[1]
    row = lambda n: pl.BlockSpec((TM, n), lambda i: (i, 0))
    one = lambda n: pl.BlockSpec((1, n), lambda i: (0, 0))
    vec = lambda v: v.reshape(1, wd).astype(F32)
    return pl.pallas_call(
        _outproj_kernel,
        out_shape=jax.ShapeDtypeStruct((t, d), F32),
        grid=(nblk,),
        in_specs=[row(d),
                  pl.BlockSpec((None, None, 1, d), lambda i: (row_of_block(i), 5, 0, 0)),
                  row(wd),
                  pl.BlockSpec((TM, wd), lambda i: (i, u_blk)),
                  one(wd),
                  pl.BlockSpec((wd, wd), lambda i: (0, 0)),
                  one(wd),
                  row(wd), row(wd), row(wd),
                  pl.BlockSpec((TM, wd), lambda i: (i, z_blk)),
                  one(wd), one(wd),
                  row(wd), row(wd),
                  pl.BlockSpec((d, d), lambda i: (0, 0))],
        out_specs=row(d),
        compiler_params=_cp(("parallel",), 48),
        name="out_projection",
    )(x, mod4, ys5, p, vec(d_s5), glu_w.astype(BF16), vec(glu_b), yf, yb, xbc, p,
      vec(jnp.repeat(d_ssd.astype(F32), wd // d_ssd.shape[0])), vec(nw_ssd), ymla, ygqa, w_out)


def _final_norm_kernel(x_ref, w_ref, o_ref):
    o_ref[...] = _rms(x_ref[...], w_ref[...])


def final_norm(x, nrows, w):
    d = x.shape[1]
    return pl.pallas_call(
        _final_norm_kernel,
        out_shape=jax.ShapeDtypeStruct((nrows, d), F32),
        grid=(nrows // TM,),
        in_specs=[pl.BlockSpec((TM, d), lambda i: (i, 0)), pl.BlockSpec((1, d), lambda i: (0, 0))],
        out_specs=pl.BlockSpec((TM, d), lambda i: (i, 0)),
        compiler_params=_cp(("parallel",), 32),
        name="final_norm",
    )(x, w.reshape(1, d).astype(F32))


COL_XBC, COL_GQA, COL_S5, COL_Z, COL_QL, COL_KVL, COL_KR, COL_DT = 0, 1024, 2048, 2560, 3072, 3456, 3584, 3712
N_IN = 3840


def _reorder_w_in(w_in, w_s5, w_ssd, xbc_w, n_dt, ql, kvl, rope, gqa_w):
    d = w_in.shape[0]
    o_ssd = w_s5
    o_mla = o_ssd + w_ssd + xbc_w + n_dt
    o_gqa = o_mla + ql + kvl + rope
    z = lambda n: jnp.zeros((d, n), w_in.dtype)
    parts = [w_in[:, o_ssd + w_ssd:o_ssd + w_ssd + xbc_w],
             w_in[:, o_gqa:o_gqa + gqa_w],
             w_in[:, 0:w_s5],
             w_in[:, o_ssd:o_ssd + w_ssd],
             w_in[:, o_mla:o_mla + ql],
             w_in[:, o_mla + ql:o_mla + ql + kvl],
             w_in[:, o_mla + ql + kvl:o_mla + ql + kvl + rope], z(LANE - rope),
             w_in[:, o_ssd + w_ssd + xbc_w:o_ssd + w_ssd + xbc_w + n_dt], z(LANE - n_dt)]
    out = jnp.concatenate(parts, axis=1)
    assert out.shape[1] == N_IN
    return out.astype(BF16)


def kernel(x, c, ctx, c_ctx, w_ada, b_ada, norm_w, ffn_w1, ffn_w3, ffn_w2, w_in, w_out, s5_lam_re, s5_lam_im, s5_log_dt, s5_b_re, s5_b_im, s5_c_re, s5_c_im, s5_d, s5_glu_w, s5_glu_b, ssd_conv_w, ssd_conv_b, ssd_dt_bias, ssd_a_log, ssd_d, ssd_norm_w, mla_q_norm_w, mla_w_qb, mla_kv_norm_w, mla_w_kvb, gqa_q_norm_w, gqa_k_norm_w, norm_f):
    nb, l_lat, d = x.shape
    l_ctx = ctx.shape[1]
    depth = w_ada.shape[0]
    nl, nc = nb * l_lat, nb * l_ctx
    t = nl + nc
    assert l_ctx == TP and nc == TM and l_lat % TM == 0 and l_lat % GRID_W == 0 and nb < SUBLANE
    assert SUBLANE % nb == 0 and (l_ctx // S5_CHUNK * nb) % SUBLANE == 0 and (l_lat // S5_CHUNK * nb) % SUBLANE == 0
    w_s5 = s5_d.shape[1]
    w_ssd = ssd_norm_w.shape[1]
    xbc_w = ssd_conv_w.shape[2]
    nheads = ssd_a_log.shape[2]
    ql, kvl = mla_q_norm_w.shape[1], mla_kv_norm_w.shape[1]
    gqa_hd = gqa_q_norm_w.shape[1]
    gqa_w = (GQA_HEADS + 2 * GQA_KV_HEADS) * gqa_hd
    assert (w_s5, w_ssd, xbc_w, ql, kvl, gqa_w) == (512, 512, 1024, 384, 128, 1024)

    xs = jnp.concatenate([x.reshape(nl, d), ctx.reshape(nc, d)], axis=0)
    act_in = jnp.concatenate([c, c_ctx[None, :], jnp.zeros((SUBLANE - nb - 1, d), F32)], axis=0)
    mod = ada_modulation(act_in, w_ada, b_ada)
    blocks_per_batch = l_lat // TM
    n_lat_blk = nl // TM
    row_of_block = lambda i: jnp.where(i < n_lat_blk, i // blocks_per_batch, nb)
    cos_m, sin_m = rope_tables(l_lat, MLA_ROPE, nb, l_ctx)
    cos_g, sin_g = rope_tables(l_lat, gqa_hd, nb, l_ctx)

    for l in range(depth):
        need_ctx = l < depth - 1
        mod4 = mod[l].reshape(SUBLANE, N_MOD, 1, d)
        nblk_all = t // TM
        nblk_tail = nblk_all if need_ctx else n_lat_blk
        w1, w3, w2 = ffn_w1[l].astype(BF16), ffn_w3[l].astype(BF16), ffn_w2[l].astype(BF16)
        xs = half_ffn(xs, nblk_all, row_of_block, mod4, 0, norm_w[l, 0], w1[0], w3[0], w2[0])
        w_in_r = _reorder_w_in(w_in[l], w_s5, w_ssd, xbc_w, 2 * nheads, ql, kvl, MLA_ROPE, gqa_w)
        p = in_projection(xs, nblk_all, row_of_block, mod4, norm_w[l, 1], w_in_r)

        tables = s5_tables(s5_lam_re[l], s5_lam_im[l], s5_log_dt[l], s5_b_re[l], s5_b_im[l], s5_c_re[l], s5_c_im[l])
        ys5 = s5_scan(p, nl, nb, l_lat, l_ctx, COL_S5 // 512, tables)

        xbc = ssd_conv(p, COL_XBC // xbc_w, ssd_conv_w[l], ssd_conv_b[l], l_lat, nl)
        dtt = p[:, COL_DT:COL_DT + 2 * nheads].T
        yf, yb = ssd_scan(xbc, p, COL_DT // LANE, dtt, ssd_dt_bias[l], ssd_a_log[l], nb, l_lat, l_ctx, nl)

        qm, km, vtm = mla_prep(p, COL_QL // ql, COL_KVL // kvl, COL_KR // LANE, mla_q_norm_w[l], mla_kv_norm_w[l],
                               mla_w_qb[l], mla_w_kvb[l], cos_m, sin_m)
        ymla = attention(qm, km, vtm, nb, l_lat, l_ctx, nl, MLA_HEADS, 1, 2 * LANE, vtm.shape[0] // MLA_HEADS, need_ctx)
        qg, kg, vtg = gqa_prep(p, COL_GQA // gqa_w, gqa_q_norm_w[l], gqa_k_norm_w[l], cos_g, sin_g)
        ygqa = attention(qg, kg, vtg, nb, l_lat, l_ctx, nl, GQA_KV_HEADS, GQA_HEADS // GQA_KV_HEADS, gqa_hd, gqa_hd, need_ctx)

        xs = out_projection(xs, nblk_tail, row_of_block, mod4, ys5, p, COL_S5 // 512, COL_Z // 512, s5_d[l], s5_glu_w[l],
                            s5_glu_b[l], yf, yb, xbc, ssd_d[l], ssd_norm_w[l], ymla, ygqa, w_out[l].astype(BF16))
        xs = half_ffn(xs, nblk_tail, row_of_block, mod4, 6, norm_w[l, 2], w1[1], w3[1], w2[1])
    return final_norm(xs, nl, norm_f).reshape(nb, l_lat, d)
```

```python
import functools
import math

import jax
import jax.numpy as jnp
from jax import lax
from jax.experimental import pallas as pl
from jax.experimental.pallas import tpu as pltpu

F32 = jnp.float32
BF16 = jnp.bfloat16
HIGHEST = lax.Precision.HIGHEST

EPS = 1e-6
ROPE_THETA = 10000.0
GRID_W = 64
N_MOD = 9
S5_GROUP = 16
S5_STATE = 64
SSD_HEADDIM = 64
SSD_GROUPS = 2
SSD_STATE = 128
SSD_CHUNK = 128
MLA_HEADS = 4
MLA_NOPE = 128
MLA_ROPE = 64
GQA_HEADS = 4
GQA_KV_HEADS = 2

LANE = 128
SUBLANE = 8
TM = 512
TP = 256
S5_CHUNK = 16
VT_PAD = 16
MIB = 1024 * 1024
LOG2E = 1.0 / math.log(2.0)
KEY_CHUNK_MAX = 1792
M_INIT = -1e30


def _cp(sem, vmem_mib):
    return pltpu.CompilerParams(dimension_semantics=sem, vmem_limit_bytes=vmem_mib * MIB)


def _sigmoid(x):
    return 1.0 / (1.0 + jnp.exp(-x))


def _silu(x):
    return x * _sigmoid(x)


def _rms(x, w):
    return x * lax.rsqrt(jnp.mean(x * x, axis=-1, keepdims=True) + EPS) * w


def _dot(a, b):
    return jnp.dot(a, b, preferred_element_type=F32)


def _dot_nt(a, b):
    return lax.dot_general(a, b, (((1,), (1,)), ((), ())), preferred_element_type=F32)


def _dot_tn(a, b):
    return lax.dot_general(a, b, (((0,), (0,)), ((), ())), preferred_element_type=F32)


def _ada_kernel(a_ref, w_ref, b_ref, o_ref):
    act = _silu(a_ref[...]).astype(BF16)
    o_ref[...] = _dot(act, w_ref[...].astype(BF16)) + b_ref[...]


def ada_modulation(act_in, w_ada, b_ada):
    depth, d, n = w_ada.shape
    tn = 1024
    return pl.pallas_call(
        _ada_kernel,
        out_shape=jax.ShapeDtypeStruct((depth, SUBLANE, n), F32),
        grid=(depth, n // tn),
        in_specs=[pl.BlockSpec((SUBLANE, d), lambda l, j: (0, 0)),
                  pl.BlockSpec((None, d, tn), lambda l, j: (l, 0, j)),
                  pl.BlockSpec((None, 1, tn), lambda l, j: (l, 0, j))],
        out_specs=pl.BlockSpec((None, SUBLANE, tn), lambda l, j: (l, 0, j)),
        compiler_params=_cp(("parallel", "arbitrary"), 40),
        name="ada_modulation",
    )(act_in, w_ada, b_ada.reshape(depth, 1, n))


def _ffn_kernel(x_ref, nw_ref, sh_ref, sc_ref, g_ref, w1_ref, w3_ref, w2_ref, *rest, final):
    fw_ref, o_ref, h_scr = rest if final else (None,) + rest
    j = pl.program_id(1)

    @pl.when(j == 0)
    def _():
        x = x_ref[...]
        h = _rms(x, nw_ref[...]) * (1.0 + sc_ref[...]) + sh_ref[...]
        h_scr[...] = h.astype(BF16)
        o_ref[...] = jnp.zeros_like(o_ref)

    h = h_scr[...]
    a = _dot(h, w1_ref[...])
    b = _dot(h, w3_ref[...])
    u = (_silu(a) * b).astype(BF16)
    o_ref[...] += _dot(u, w2_ref[...])

    @pl.when(j == pl.num_programs(1) - 1)
    def _():
        y = x_ref[...] + 0.5 * g_ref[...] * o_ref[...]
        o_ref[...] = _rms(y, fw_ref[...]) if final else y


def half_ffn(x, nblk, row_of_block, mod4, k0, norm_w, w1, w3, w2, final_w=None):
    t, d = x.shape
    f = w1.shape[1]
    tf = 512
    final = final_w is not None
    mspec = lambda k: pl.BlockSpec((None, None, 1, d), lambda i, j: (row_of_block(i), k, 0, 0))
    return pl.pallas_call(
        functools.partial(_ffn_kernel, final=final),
        out_shape=jax.ShapeDtypeStruct((nblk * TM if final else t, d), F32),
        grid=(nblk, f // tf),
        in_specs=[pl.BlockSpec((TM, d), lambda i, j: (i, 0)),
                  pl.BlockSpec((1, d), lambda i, j: (0, 0)),
                  mspec(k0), mspec(k0 + 1), mspec(k0 + 2),
                  pl.BlockSpec((d, tf), lambda i, j: (0, j)),
                  pl.BlockSpec((d, tf), lambda i, j: (0, j)),
                  pl.BlockSpec((tf, d), lambda i, j: (j, 0))] + [pl.BlockSpec((1, d), lambda i, j: (0, 0))] * final,
        out_specs=pl.BlockSpec((TM, d), lambda i, j: (i, 0)),
        scratch_shapes=[pltpu.VMEM((TM, d), BF16)],
        compiler_params=_cp(("parallel", "arbitrary"), 48),
        name="half_ffn",
    )(x, norm_w.reshape(1, d), mod4, mod4, mod4, w1, w3, w2, *([final_w.reshape(1, d).astype(F32)] if final else []))


def _inproj_kernel(x_ref, nw_ref, sh_ref, sc_ref, w_ref, o_ref, h_scr):
    @pl.when(pl.program_id(1) == 0)
    def _():
        h = _rms(x_ref[...], nw_ref[...]) * (1.0 + sc_ref[...]) + sh_ref[...]
        h_scr[...] = h.astype(BF16)

    o_ref[...] = _dot(h_scr[...], w_ref[...])


def in_projection(x, nblk, row_of_block, mod4, norm_w, w):
    t, d = x.shape
    n = w.shape[1]
    tn = 1280
    mspec = lambda k: pl.BlockSpec((None, None, 1, d), lambda i, j: (row_of_block(i), k, 0, 0))
    return pl.pallas_call(
        _inproj_kernel,
        out_shape=jax.ShapeDtypeStruct((t, n), F32),
        grid=(nblk, n // tn),
        in_specs=[pl.BlockSpec((TM, d), lambda i, j: (i, 0)),
                  pl.BlockSpec((1, d), lambda i, j: (0, 0)),
                  mspec(3), mspec(4),
                  pl.BlockSpec((d, tn), lambda i, j: (0, j))],
        out_specs=pl.BlockSpec((TM, tn), lambda i, j: (i, j)),
        scratch_shapes=[pltpu.VMEM((TM, d), BF16)],
        compiler_params=_cp(("parallel", "arbitrary"), 40),
        name="in_projection",
    )(x, norm_w.reshape(1, d), mod4, mod4, w)


def s5_tables(lam_re, lam_im, log_dt, b_re, b_im, c_re, c_im):
    n = S5_CHUNK
    hp = functools.partial(jnp.einsum, precision=HIGHEST)
    step = jnp.exp(log_dt.astype(F32))[..., None]
    lr, li = lam_re.astype(F32), lam_im.astype(F32)
    jj = jnp.arange(n + 1, dtype=F32)
    mag = jnp.exp(lr[..., None] * step[..., None] * jj)
    ang = li[..., None] * step[..., None] * jj
    pr, pi = mag * jnp.cos(ang), mag * jnp.sin(ang)
    ar, ai = pr[..., 1], pi[..., 1]
    den = lr * lr + li * li
    qr = ((ar - 1.0) * lr + ai * li) / den
    qi = (ai * lr - (ar - 1.0) * li) / den
    br, bi = b_re.astype(F32), b_im.astype(F32)
    bbr = qr[..., None] * br - qi[..., None] * bi
    bbi = qr[..., None] * bi + qi[..., None] * br
    cr, ci = c_re.astype(F32), c_im.astype(F32)
    k = (hp('dgop,dgpj,dgpi->dgjoi', cr, pr[..., :n], bbr) - hp('dgop,dgpj,dgpi->dgjoi', cr, pi[..., :n], bbi)
         - hp('dgop,dgpj,dgpi->dgjoi', ci, pr[..., :n], bbi) - hp('dgop,dgpj,dgpi->dgjoi', ci, pi[..., :n], bbr))
    g = k.shape[1]
    h = k.shape[-1]
    s_idx = jnp.arange(n)[:, None]
    r_idx = jnp.arange(n)[None, :]
    lag = r_idx - s_idx
    kf = jnp.where((lag >= 0)[None, :, :, None, None], k[0][:, jnp.clip(lag, 0, n - 1)], 0.0)
    kb = jnp.where((lag <= 0)[None, :, :, None, None], k[1][:, jnp.clip(-lag, 0, n - 1)], 0.0)
    toep = jnp.transpose(kf + kb, (0, 1, 4, 2, 3)).reshape(g, n * h, n * h)

    def bend(d, pw_r, pw_i):
        re = hp('gps,gpi->gsip', pw_r, bbr[d]) - hp('gps,gpi->gsip', pw_i, bbi[d])
        im = hp('gps,gpi->gsip', pw_r, bbi[d]) + hp('gps,gpi->gsip', pw_i, bbr[d])
        return re.reshape(g, n * h, -1), im.reshape(g, n * h, -1)

    bf_r, bf_i = bend(0, pr[0][..., n - 1::-1][..., :n], pi[0][..., n - 1::-1][..., :n])
    bb_r, bb_i = bend(1, pr[1][..., :n], pi[1][..., :n])

    def coff(d, pw_r, pw_i):
        re = hp('gop,gpr->gpro', cr[d], pw_r) - hp('gop,gpr->gpro', ci[d], pw_i)
        im = -(hp('gop,gpr->gpro', cr[d], pw_i) + hp('gop,gpr->gpro', ci[d], pw_r))
        return re.reshape(g, -1, n * h), im.reshape(g, -1, n * h)

    cf_r, cf_i = coff(0, pr[0][..., 1:], pi[0][..., 1:])
    cb_r, cb_i = coff(1, pr[1][..., n:0:-1], pi[1][..., n:0:-1])

    q = g // 2
    w = n * h
    p = pr.shape[2]

    def pair_diag(m):
        m = m.reshape(q, 2, m.shape[1], m.shape[2])
        z = jnp.zeros_like(m[:, 0])
        return jnp.concatenate([jnp.concatenate([m[:, 0], z], axis=2), jnp.concatenate([z, m[:, 1]], axis=2)], axis=1)

    def step_major(m, axis):
        shp = m.shape
        m = m.reshape(shp[:axis] + (2, n, h) + shp[axis + 1:])
        return jnp.swapaxes(m, axis, axis + 1).reshape(shp)

    toep_p = step_major(step_major(pair_diag(toep), 1), 2)
    bend_p = jnp.concatenate([pair_diag(bf_r), pair_diag(bf_i), pair_diag(bb_r), pair_diag(bb_i)], axis=2)
    bend_p = step_major(bend_p, 1)
    coff_p = jnp.concatenate([pair_diag(cf_r), pair_diag(cf_i), pair_diag(cb_r), pair_diag(cb_i)], axis=1)
    coff_p = step_major(coff_p, 2)
    a16 = jnp.stack([pr[0][..., n], pi[0][..., n], pr[1][..., n], pi[1][..., n]], axis=1)
    a16 = a16.reshape(q, 2, 4, p).transpose(0, 2, 1, 3).reshape(q, 4, 2 * p)
    a16 = jnp.concatenate([a16, jnp.zeros((q, 4, 2 * p), F32)], axis=1)
    return toep_p.astype(BF16), bend_p.astype(BF16), coff_p.astype(BF16), a16


def _s5_local_kernel(u_ref, toep_ref, bend_ref, y_ref, h_ref):
    u = u_ref[...]
    y_ref[...] = _dot(u, toep_ref[...])
    h_ref[...] = _dot(u, bend_ref[...])


def _s5_carry_kernel(a_ref, hl_ref, hs_ref, *, nb, n_ctx_tiles, n_lat_tiles):
    a = a_ref[...]
    afr, afi, abr, abi = a[0:1], a[1:2], a[2:3], a[3:4]
    spt = SUBLANE // nb

    def tile_pass(tile_f, tile_b, carry):
        fr, fi, br, bi = carry
        rf = pl.multiple_of(tile_f * SUBLANE, SUBLANE)
        rb = pl.multiple_of(tile_b * SUBLANE, SUBLANE)
        lf = hl_ref[pl.ds(rf, SUBLANE), 0:2 * LANE]
        lb = hl_ref[pl.ds(rb, SUBLANE), 2 * LANE:4 * LANE]
        of_r, of_i, ob_r, ob_i = [], [], [], []
        for k in range(spt):
            of_r.append(fr)
            of_i.append(fi)
            lo = k * nb
            nr = afr * fr - afi * fi + lf[lo:lo + nb, 0:LANE]
            ni = afr * fi + afi * fr + lf[lo:lo + nb, LANE:2 * LANE]
            fr, fi = nr, ni
        for k in range(spt - 1, -1, -1):
            ob_r.append(br)
            ob_i.append(bi)
            lo = k * nb
            nr = abr * br - abi * bi + lb[lo:lo + nb, 0:LANE]
            ni = abr * bi + abi * br + lb[lo:lo + nb, LANE:2 * LANE]
            br, bi = nr, ni
        hs_ref[pl.ds(rf, SUBLANE), 0:LANE] = jnp.concatenate(of_r, axis=0)
        hs_ref[pl.ds(rf, SUBLANE), LANE:2 * LANE] = jnp.concatenate(of_i, axis=0)
        hs_ref[pl.ds(rb, SUBLANE), 2 * LANE:3 * LANE] = jnp.concatenate(ob_r[::-1], axis=0)
        hs_ref[pl.ds(rb, SUBLANE), 3 * LANE:4 * LANE] = jnp.concatenate(ob_i[::-1], axis=0)
        return fr, fi, br, bi

    z = jnp.zeros((nb, LANE), F32)
    carry = (z, z, z, z)
    carry = lax.fori_loop(0, n_ctx_tiles, lambda t, c: tile_pass(t, n_ctx_tiles - 1 - t, c), carry)
    lax.fori_loop(0, n_lat_tiles,
                  lambda t, c: tile_pass(n_ctx_tiles + t, n_ctx_tiles + n_lat_tiles - 1 - t, c), carry)


def _s5_out_kernel(yi_ref, hs_ref, coff_ref, y_ref):
    y_ref[...] = yi_ref[...] + _dot(hs_ref[...].astype(BF16), coff_ref[...])


def s5_scan(p, nl, nb, l_lat, l_ctx, col_blk, tables):
    toep, bend, coff, a16 = tables
    q = toep.shape[0]
    w = 2 * S5_CHUNK * S5_GROUP
    t = p.shape[0]
    u = p[:, col_blk * w:(col_blk + 1) * w].astype(BF16)
    c_lat, c_ctx = l_lat // S5_CHUNK, l_ctx // S5_CHUNK
    wp = 2 * S5_GROUP

    def to_chunks(v, nchunk):
        v = v.reshape(nb, nchunk, S5_CHUNK, q, wp)
        return jnp.transpose(v, (3, 1, 0, 2, 4)).reshape(q, nchunk * nb, w)

    def from_chunks(v, nchunk):
        v = v.reshape(q, nchunk, nb, S5_CHUNK, wp)
        return jnp.transpose(v, (2, 1, 3, 0, 4)).reshape(nb * nchunk * S5_CHUNK, w)

    uc = jnp.concatenate([to_chunks(u[nl:], c_ctx), to_chunks(u[:nl], c_lat)], axis=1)
    nrow = uc.shape[1]
    yi, hl = pl.pallas_call(
        _s5_local_kernel,
        out_shape=(jax.ShapeDtypeStruct((q, nrow, w), F32), jax.ShapeDtypeStruct((nrow, q * w), F32)),
        grid=(q,),
        in_specs=[pl.BlockSpec((None, nrow, w), lambda g: (g, 0, 0)),
                  pl.BlockSpec((None, w, w), lambda g: (g, 0, 0)),
                  pl.BlockSpec((None, w, w), lambda g: (g, 0, 0))],
        out_specs=(pl.BlockSpec((None, nrow, w), lambda g: (g, 0, 0)),
                   pl.BlockSpec((nrow, w), lambda g: (0, g))),
        compiler_params=_cp(("parallel",), 40),
        name="s5_local",
    )(uc, toep, bend)
    hs = pl.pallas_call(
        functools.partial(_s5_carry_kernel, nb=nb, n_ctx_tiles=c_ctx * nb // SUBLANE,
                          n_lat_tiles=c_lat * nb // SUBLANE),
        out_shape=jax.ShapeDtypeStruct((nrow, q * w), F32),
        grid=(q,),
        in_specs=[pl.BlockSpec((None, SUBLANE, LANE), lambda g: (g, 0, 0)),
                  pl.BlockSpec((nrow, w), lambda g: (0, g))],
        out_specs=pl.BlockSpec((nrow, w), lambda g: (0, g)),
        compiler_params=_cp(("parallel",), 40),
        name="s5_carry",
    )(a16, hl)
    y = pl.pallas_call(
        _s5_out_kernel,
        out_shape=jax.ShapeDtypeStruct((q, nrow, w), F32),
        grid=(q,),
        in_specs=[pl.BlockSpec((None, nrow, w), lambda g: (g, 0, 0)),
                  pl.BlockSpec((nrow, w), lambda g: (0, g)),
                  pl.BlockSpec((None, w, w), lambda g: (g, 0, 0))],
        out_specs=pl.BlockSpec((None, nrow, w), lambda g: (g, 0, 0)),
        compiler_params=_cp(("parallel",), 40),
        name="s5_out",
    )(yi, hs, coff)
    n_ctx_rows = c_ctx * nb
    return jnp.concatenate([from_chunks(y[:, n_ctx_rows:], c_lat), from_chunks(y[:, :n_ctx_rows], c_ctx)], axis=0)


def _ssd_conv_kernel(x_ref, prev_ref, next_ref, w_ref, b_ref, o_ref, pad_scr, *, blocks_per_seq, n_lat_blocks):
    i = pl.program_id(0)
    is_lat = i < n_lat_blocks
    first = jnp.logical_or(jnp.logical_not(is_lat), i % blocks_per_seq == 0)
    last = jnp.logical_or(jnp.logical_not(is_lat), (i + 1) % blocks_per_seq == 0)
    tp = x_ref.shape[0]
    pad_scr[0:SUBLANE, :] = jnp.where(first, 0.0, prev_ref[...])
    pad_scr[SUBLANE:SUBLANE + tp, :] = x_ref[...]
    pad_scr[SUBLANE + tp:2 * SUBLANE + tp, :] = jnp.where(last, 0.0, next_ref[...])
    w = w_ref[...]
    kk = 5
    acc = b_ref[...] + w[0:1] * pad_scr[SUBLANE - kk // 2:SUBLANE - kk // 2 + tp, :]
    for k in range(1, kk):
        off = SUBLANE - kk // 2 + k
        acc = acc + w[k:k + 1] * pad_scr[off:off + tp, :]
    o_ref[...] = _silu(acc)


def ssd_conv(p, col_blk, conv_w, conv_b, l_lat, nl):
    t = p.shape[0]
    c = conv_w.shape[1]
    assert conv_w.shape[0] == 5
    r = TP // SUBLANE
    wpad = jnp.concatenate([conv_w.astype(F32), jnp.zeros((SUBLANE - conv_w.shape[0], c), F32)], axis=0)
    nsub = t // SUBLANE
    return pl.pallas_call(
        functools.partial(_ssd_conv_kernel, blocks_per_seq=l_lat // TP, n_lat_blocks=nl // TP),
        out_shape=jax.ShapeDtypeStruct((t, c), F32),
        grid=(t // TP,),
        in_specs=[pl.BlockSpec((TP, c), lambda i: (i, col_blk)),
                  pl.BlockSpec((SUBLANE, c), lambda i: (jnp.maximum(i * r - 1, 0), col_blk)),
                  pl.BlockSpec((SUBLANE, c), lambda i: (jnp.minimum((i + 1) * r, nsub - 1), col_blk)),
                  pl.BlockSpec((SUBLANE, c), lambda i: (0, 0)),
                  pl.BlockSpec((1, c), lambda i: (0, 0))],
        out_specs=pl.BlockSpec((TP, c), lambda i: (i, 0)),
        scratch_shapes=[pltpu.VMEM((TP + 2 * SUBLANE, c), F32)],
        compiler_params=_cp(("parallel",), 32),
        name="ssd_conv",
    )(p, p, p, wpad, conv_b.reshape(1, c).astype(F32))


def _softplus(x):
    return jnp.maximum(x, 0.0) + jnp.log(1.0 + jnp.exp(-jnp.abs(x)))


def _split_bf16(x, terms):
    out = []
    for _ in range(terms):
        piece = x.astype(BF16)
        out.append(piece)
        x = x - piece.astype(F32)
    return out


def _ssd_chunk(xbc_ref, dt_ref, dtt_ref, bias, biast, a_row, a_col, e, st_ref, *, reverse, lane0, nheads, hd, ns,
               ngroups):
    tc = xbc_ref.shape[0]
    wx = nheads * hd
    hpg = nheads // ngroups
    x = xbc_ref[:, 0:wx]
    rows = lax.broadcasted_iota(jnp.int32, (tc, tc), 0)
    cols = lax.broadcasted_iota(jnp.int32, (tc, tc), 1)
    keep = (cols >= rows) if reverse else (cols <= rows)
    tri = keep.astype(BF16)
    tri_t = ((rows >= cols) if reverse else (rows <= cols)).astype(BF16)
    dt = _softplus(dt_ref[...] + bias)
    acs = sum(_dot(tri, piece) for piece in _split_bf16(dt * a_row, 3))
    dtt = _softplus(dtt_ref[...] + biast)
    acst = sum(_dot(piece, tri_t) for piece in _split_bf16(dtt * a_col, 3))
    edge = tc - 1 if not reverse else 0
    tot = acs[edge:edge + 1, :]
    fac = jnp.concatenate([dt, jnp.exp(acs), jnp.exp(tot - acs)], axis=0)
    spread = sum(_dot(piece, e) for piece in _split_bf16(fac, 2))
    dtx, eacs, edec = spread[0:tc], spread[tc:2 * tc], spread[2 * tc:3 * tc]
    chunk_decay = eacs[edge:edge + 1, :]
    xd = x * dtx
    xdb = xd.astype(BF16)
    xdd = (xd * edec).astype(BF16)
    ys = []
    for g in range(ngroups):
        bg = xbc_ref[:, wx + g * ns:wx + (g + 1) * ns].astype(BF16)
        cg = xbc_ref[:, wx + (ngroups + g) * ns:wx + (ngroups + g + 1) * ns].astype(BF16)
        cb = _dot_nt(cg, bg)
        lo, hi = g * hpg * hd, (g + 1) * hpg * hd
        st = st_ref[:, lo:hi]
        y_off = _dot(cg, st.astype(BF16)) * eacs[:, lo:hi]
        st_ref[:, lo:hi] = st * chunk_decay[:, lo:hi] + _dot_tn(bg, xdd[:, lo:hi])
        yd = []
        for hh in range(hpg):
            h = g * hpg + hh
            col = acs[:, lane0 + h:lane0 + h + 1]
            row = acst[lane0 + h:lane0 + h + 1, :]
            lm = jnp.exp(jnp.where(keep, col - row, -jnp.inf))
            yd.append(_dot((cb * lm).astype(BF16), xdb[:, h * hd:(h + 1) * hd]))
        ys.append(jnp.concatenate(yd, axis=1) + y_off)
    return jnp.concatenate(ys, axis=1)


def _ssd_scan_kernel(xf_ref, dtf_ref, dttf_ref, xb_ref, dtb_ref, dttb_ref, bias_ref, biast_ref, a_ref, at_ref,
                     ef_ref, eb_ref, yf_ref, yb_ref, st_scr, *, nheads, hd, ns, ngroups):
    @pl.when(pl.program_id(1) == 0)
    def _():
        st_scr[...] = jnp.zeros_like(st_scr)

    kw = dict(nheads=nheads, hd=hd, ns=ns, ngroups=ngroups)
    bias, biast, a_row, a_col = bias_ref[...], biast_ref[...], a_ref[...], at_ref[...]
    yf_ref[...] = _ssd_chunk(xf_ref, dtf_ref, dttf_ref, bias, biast, a_row, a_col, ef_ref[...], st_scr.at[0],
                             reverse=False, lane0=0, **kw)
    yb_ref[...] = _ssd_chunk(xb_ref, dtb_ref, dttb_ref, bias, biast, a_row, a_col, eb_ref[...], st_scr.at[1],
                             reverse=True, lane0=nheads, **kw)


def ssd_scan(xbc, p, dt_blk, dtt, dt_bias, a_log, nb, l_lat, l_ctx, nl):
    t = xbc.shape[0]
    nheads = a_log.shape[1]
    hd = SSD_HEADDIM
    wx = nheads * hd
    tc = SSD_CHUNK
    nch_ctx, nch_lat = l_ctx // tc, l_lat // tc
    nch = nch_ctx + nch_lat
    a = -jnp.exp(a_log.astype(F32))
    a_row = jnp.zeros((1, LANE), F32).at[0, :2 * nheads].set(a.reshape(-1))
    bias_row = jnp.zeros((1, LANE), F32).at[0, :2 * nheads].set(dt_bias.astype(F32).reshape(-1))
    a_col = a.reshape(2 * nheads, 1)
    bias_col = dt_bias.astype(F32).reshape(2 * nheads, 1)

    def expand(direction):
        hidx = jnp.arange(LANE)[:, None] - direction * nheads
        return (hidx == (jnp.arange(wx)[None, :] // hd)).astype(BF16)

    def chunk_block(reverse):
        def f(b, k):
            is_ctx = k < nch_ctx
            ctx_c = (nch_ctx - 1 - k) if reverse else k
            lat_c = (nch_lat - 1 - (k - nch_ctx)) if reverse else (k - nch_ctx)
            return jnp.where(is_ctx, (nl + b * l_ctx) // tc + ctx_c, b * nch_lat + lat_c)
        return f

    blk_f, blk_b = chunk_block(False), chunk_block(True)
    chunk_specs = lambda blk: [pl.BlockSpec((tc, xbc.shape[1]), lambda b, k: (blk(b, k), 0)),
                               pl.BlockSpec((tc, LANE), lambda b, k: (blk(b, k), dt_blk)),
                               pl.BlockSpec((2 * nheads, tc), lambda b, k: (0, blk(b, k)))]
    const = lambda shape: pl.BlockSpec(shape, lambda b, k: (0, 0))
    return pl.pallas_call(
        functools.partial(_ssd_scan_kernel, nheads=nheads, hd=hd, ns=SSD_STATE, ngroups=SSD_GROUPS),
        out_shape=(jax.ShapeDtypeStruct((t, wx), F32), jax.ShapeDtypeStruct((t, wx), F32)),
        grid=(nb, nch),
        in_specs=chunk_specs(blk_f) + chunk_specs(blk_b) + [const((1, LANE)), const((2 * nheads, 1)), const((1, LANE)),
                                                            const((2 * nheads, 1)), const((LANE, wx)), const((LANE, wx))],
        out_specs=(pl.BlockSpec((tc, wx), lambda b, k: (blk_f(b, k), 0)),
                   pl.BlockSpec((tc, wx), lambda b, k: (blk_b(b, k), 0))),
        scratch_shapes=[pltpu.VMEM((2, SSD_STATE, wx), F32)],
        compiler_params=_cp(("parallel", "arbitrary"), 32),
        name="ssd_scan",
    )(xbc, p, dtt, xbc, p, dtt, bias_row, bias_col, a_row, a_col, expand(0), expand(1))


def rope_tables(l_lat, rot_dim, nb, l_ctx):
    rows = l_lat // GRID_W
    row = jnp.repeat(jnp.arange(rows, dtype=F32), GRID_W)
    col = jnp.tile(jnp.arange(GRID_W, dtype=F32), rows)
    n_freq = rot_dim // 4
    inv_freq = ROPE_THETA ** (-jnp.arange(n_freq, dtype=F32) / n_freq)
    ang = jnp.concatenate([row[:, None] * inv_freq, col[:, None] * inv_freq], axis=-1)
    cos = jnp.concatenate([jnp.tile(jnp.cos(ang), (nb, 1)), jnp.ones((nb * l_ctx, rot_dim // 2), F32)], axis=0)
    sin = jnp.concatenate([jnp.tile(jnp.sin(ang), (nb, 1)), jnp.zeros((nb * l_ctx, rot_dim // 2), F32)], axis=0)
    return cos, sin


def _store_values_t(vt_ref, vt, nheads):
    dv = vt.shape[0] // nheads
    tail = (lax.broadcasted_iota(jnp.int32, (VT_PAD, vt.shape[1]), 0) == 0).astype(BF16)
    for h in range(nheads):
        vt_ref[h * (dv + VT_PAD):h * (dv + VT_PAD) + dv, :] = vt[h * dv:(h + 1) * dv].astype(BF16)
        vt_ref[h * (dv + VT_PAD) + dv:(h + 1) * (dv + VT_PAD), :] = tail


def _mla_prep_kernel(ql_ref, kvl_ref, kr_ref, qnw_ref, kvnw_ref, wq_ref, wk_ref, wvt_ref, c_ref, s1_ref, s2_ref,
                     q_ref, k_ref, vt_ref, *, nheads, scale):
    qn = _rms(ql_ref[...], qnw_ref[...]).astype(BF16)
    q = _dot(qn, wq_ref[...])
    ckv = _rms(kvl_ref[...], kvnw_ref[...]).astype(BF16)
    kn = _dot(ckv, wk_ref[...])
    _store_values_t(vt_ref, _dot_nt(wvt_ref[...], ckv), nheads)
    c, s1, s2 = c_ref[...], s1_ref[...], s2_ref[...]

    def rope(v):
        return v * c + pltpu.roll(v, 3 * LANE // 4, 1) * s1 + pltpu.roll(v, LANE // 4, 1) * s2

    kr = rope(kr_ref[...]).astype(BF16)
    for h in range(nheads):
        b0 = 2 * LANE * h
        q_ref[:, b0:b0 + LANE] = (q[:, b0:b0 + LANE] * scale).astype(BF16)
        q_ref[:, b0 + LANE:b0 + 2 * LANE] = (rope(q[:, b0 + LANE:b0 + 2 * LANE]) * scale).astype(BF16)
        k_ref[:, b0:b0 + LANE] = kn[:, LANE * h:LANE * (h + 1)].astype(BF16)
        k_ref[:, b0 + LANE:b0 + 2 * LANE] = kr


def mla_prep(p, blk_q, blk_kv, blk_kr, q_norm_w, kv_norm_w, w_qb, w_kvb, cos, sin, seq_blk):
    t = p.shape[0]
    nh = MLA_HEADS
    qk = MLA_NOPE + MLA_ROPE
    ql, kvl = q_norm_w.shape[0], kv_norm_w.shape[0]
    assert MLA_NOPE == LANE and MLA_ROPE == LANE // 2 and kvl == LANE
    dv = w_kvb.shape[1] // nh - MLA_NOPE
    wq = w_qb.reshape(ql, nh, qk)
    wq = jnp.concatenate([wq, jnp.zeros((ql, nh, 2 * LANE - qk), w_qb.dtype)], axis=-1).reshape(ql, nh * 2 * LANE)
    wkv = w_kvb.reshape(kvl, nh, MLA_NOPE + dv)
    wk = wkv[:, :, :MLA_NOPE].reshape(kvl, nh * MLA_NOPE)
    wvt = wkv[:, :, MLA_NOPE:].reshape(kvl, nh * dv).T
    half = MLA_ROPE // 2
    zeros = jnp.zeros((t, half), F32)
    c_tab = jnp.concatenate([cos, cos, jnp.ones((t, 2 * half), F32)], axis=1)
    s1_tab = jnp.concatenate([-sin, zeros, zeros, zeros], axis=1)
    s2_tab = jnp.concatenate([zeros, sin, zeros, zeros], axis=1)
    row = lambda w: pl.BlockSpec((TP, w), lambda i: (i, 0))
    full = lambda a: pl.BlockSpec(a.shape, lambda i: (0, 0))
    wq, wk, wvt = wq.astype(BF16), wk.astype(BF16), wvt.astype(BF16)
    qnw, kvnw = q_norm_w.reshape(1, ql).astype(F32), kv_norm_w.reshape(1, kvl).astype(F32)
    return pl.pallas_call(
        functools.partial(_mla_prep_kernel, nheads=nh, scale=float(qk) ** -0.5 * LOG2E),
        out_shape=(jax.ShapeDtypeStruct((t, nh * 2 * LANE), BF16), jax.ShapeDtypeStruct((t, nh * 2 * LANE), BF16),
                   jax.ShapeDtypeStruct((nh * (dv + VT_PAD), t), BF16)),
        grid=(t // TP,),
        in_specs=[pl.BlockSpec((TP, ql), lambda i: (i, blk_q)),
                  pl.BlockSpec((TP, kvl), lambda i: (i, blk_kv)),
                  pl.BlockSpec((TP, LANE), lambda i: (i, blk_kr)),
                  full(qnw), full(kvnw), full(wq), full(wk), full(wvt), row(LANE), row(LANE), row(LANE)],
        out_specs=(row(nh * 2 * LANE), pl.BlockSpec((TP, nh * 2 * LANE), lambda i: (seq_blk(i), 0)),
                   pl.BlockSpec((nh * (dv + VT_PAD), TP), lambda i: (0, seq_blk(i)))),
        compiler_params=_cp(("parallel",), 32),
        name="mla_prep",
    )(p, p, p, qnw, kvnw, wq, wk, wvt, c_tab, s1_tab, s2_tab)


def _gqa_prep_kernel(pg_ref, qnw_ref, knw_ref, c_ref, s_ref, q_ref, k_ref, vt_ref, *, nq, nkv, scale):
    c, s = c_ref[...], s_ref[...]

    def norm_rope(v, w):
        n = _rms(v, w)
        return n * c + pltpu.roll(n, LANE // 2, 1) * s

    for h in range(nq):
        q_ref[:, h * LANE:(h + 1) * LANE] = (norm_rope(pg_ref[:, h * LANE:(h + 1) * LANE], qnw_ref[...]) * scale).astype(BF16)
    for h in range(nkv):
        o = (nq + h) * LANE
        k_ref[:, h * LANE:(h + 1) * LANE] = norm_rope(pg_ref[:, o:o + LANE], knw_ref[...]).astype(BF16)
    o = (nq + nkv) * LANE
    _store_values_t(vt_ref, pg_ref[:, o:o + nkv * LANE].T, nkv)


def gqa_prep(p, blk, q_norm_w, k_norm_w, cos, sin, seq_blk):
    t = p.shape[0]
    nq, nkv = GQA_HEADS, GQA_KV_HEADS
    hd = q_norm_w.shape[0]
    assert hd == LANE
    w = (nq + 2 * nkv) * hd
    c_tab = jnp.concatenate([cos, cos], axis=1)
    s_tab = jnp.concatenate([-sin, sin], axis=1)
    row = lambda n: pl.BlockSpec((TP, n), lambda i: (i, 0))
    one = pl.BlockSpec((1, hd), lambda i: (0, 0))
    return pl.pallas_call(
        functools.partial(_gqa_prep_kernel, nq=nq, nkv=nkv, scale=float(hd) ** -0.5 * LOG2E),
        out_shape=(jax.ShapeDtypeStruct((t, nq * hd), BF16), jax.ShapeDtypeStruct((t, nkv * hd), BF16),
                   jax.ShapeDtypeStruct((nkv * (hd + VT_PAD), t), BF16)),
        grid=(t // TP,),
        in_specs=[pl.BlockSpec((TP, w), lambda i: (i, blk)), one, one, row(hd), row(hd)],
        out_specs=(row(nq * hd), pl.BlockSpec((TP, nkv * hd), lambda i: (seq_blk(i), 0)),
                   pl.BlockSpec((nkv * (hd + VT_PAD), TP), lambda i: (0, seq_blk(i)))),
        compiler_params=_cp(("parallel",), 32),
        name="gqa_prep",
    )(p, q_norm_w.reshape(1, hd).astype(F32), k_norm_w.reshape(1, hd).astype(F32), c_tab, s_tab)


def _flash_kernel(*refs, rep, rows, dk, dv, tk, n_chunks, aliased):
    if aliased:
        refs = refs[1:]
    if n_chunks > 1:
        q_ref, k_ref, vt_ref, o_ref, m_scr, acc_scr, s_scr = refs
    else:
        q_ref, k_ref, vt_ref, o_ref, m_scr, acc_scr = refs
    tq = q_ref.shape[0] // rows
    streams = [(h, r) for h in range(rows) for r in range(rep)]
    qs = [q_ref[h * tq:(h + 1) * tq, r * dk:(r + 1) * dk] for h, r in streams]

    def chunk_softmax(st, vt):
        mc = jnp.max(st, axis=0, keepdims=True)
        pt = jnp.exp2(st - mc).astype(BF16)
        return mc, _dot(vt, pt)

    def merge(s, mc, pv):
        m_old = m_scr[s]
        m_new = jnp.maximum(m_old, mc)
        m_scr[s] = m_new
        acc_scr[s] = jnp.exp2(m_old - m_new) * acc_scr[s] + jnp.exp2(mc - m_new) * pv

    m_scr[...] = jnp.full(m_scr.shape, M_INIT, F32)
    acc_scr[...] = jnp.zeros_like(acc_scr)
    if n_chunks == 1:
        for s, q in enumerate(qs):
            merge(s, *chunk_softmax(_dot_nt(k_ref[...], q), vt_ref[...]))
    else:
        for s, q in enumerate(qs):
            s_scr[s, 0] = _dot_nt(k_ref[0:tk, :], q)

        def step(j, cur):
            off_next = pl.multiple_of(jnp.minimum(j + 1, n_chunks - 1) * tk, tk)
            off = pl.multiple_of(j * tk, tk)
            for s, q in enumerate(qs):
                s_scr[s, 1 - cur] = _dot_nt(k_ref[pl.ds(off_next, tk), :], q)
                merge(s, *chunk_softmax(s_scr[s, cur], vt_ref[:, pl.ds(off, tk)]))

        def body(i, carry):
            step(2 * i, 0)
            step(2 * i + 1, 1)
            return carry

        lax.fori_loop(0, n_chunks // 2, body, 0)
    for s, (h, r) in enumerate(streams):
        out = acc_scr[s, 0:dv, :] / acc_scr[s, dv:dv + 1, :]
        o_ref[h * tq:(h + 1) * tq, r * dv:(r + 1) * dv] = out.T.astype(o_ref.dtype)


def _key_chunk(s_len):
    for tk in range(KEY_CHUNK_MAX, 0, -LANE):
        if s_len % tk == 0 and (s_len // tk) % 2 == 0:
            return tk
    raise ValueError(f"no key chunk for sequence length {s_len}")


def attention(q, k, vt, nb, l_lat, l_ctx, nl, hkv, rep, dk, dv, need_ctx):
    t = q.shape[0]
    s_len = l_ctx + l_lat
    tq = 512 if l_lat % 512 == 0 else TP
    tk = _key_chunk(s_len)
    rows = 2 // rep if l_lat % (tq * 2 // rep) == 0 else 1
    nq = l_lat // (tq * rows)
    cb = nl // l_ctx
    spb = s_len // l_ctx
    dvx = dv + VT_PAD
    scratch = lambda ns, n: [pltpu.VMEM((ns, 1, n), F32), pltpu.VMEM((ns, dvx, n), F32)]
    o = pl.pallas_call(
        functools.partial(_flash_kernel, rep=rep, rows=rows, dk=dk, dv=dv, tk=tk, n_chunks=s_len // tk, aliased=False),
        out_shape=jax.ShapeDtypeStruct((t, hkv * rep * dv), BF16),
        grid=(nb, hkv, nq),
        in_specs=[pl.BlockSpec((tq * rows, rep * dk), lambda b, g, i: (b * nq + i, g)),
                  pl.BlockSpec((s_len, dk), lambda b, g, i: (b, g)),
                  pl.BlockSpec((dvx, s_len), lambda b, g, i: (g, b))],
        out_specs=pl.BlockSpec((tq * rows, rep * dv), lambda b, g, i: (b * nq + i, g)),
        scratch_shapes=scratch(rows * rep, tq) + [pltpu.VMEM((rows * rep, 2, tk, tq), F32)],
        compiler_params=_cp(("parallel", "parallel", "arbitrary"), 56),
        name="attention_latent",
    )(q, k, vt)
    if not need_ctx:
        return o
    return pl.pallas_call(
        functools.partial(_flash_kernel, rep=rep, rows=1, dk=dk, dv=dv, tk=l_ctx, n_chunks=1, aliased=True),
        out_shape=jax.ShapeDtypeStruct((t, hkv * rep * dv), BF16),
        grid=(nb, hkv),
        in_specs=[pl.BlockSpec(memory_space=pl.ANY),
                  pl.BlockSpec((l_ctx, rep * dk), lambda b, g: (cb + b, g)),
                  pl.BlockSpec((l_ctx, dk), lambda b, g: (b * spb, g)),
                  pl.BlockSpec((dvx, l_ctx), lambda b, g: (g, b * spb))],
        out_specs=pl.BlockSpec((l_ctx, rep * dv), lambda b, g: (cb + b, g)),
        scratch_shapes=scratch(rep, l_ctx),
        input_output_aliases={0: 0},
        compiler_params=_cp(("parallel", "parallel"), 32),
        name="attention_context",
    )(o, q, k, vt)


def _gelu_tanh(x):
    return 0.5 * x * (1.0 + jnp.tanh(math.sqrt(2.0 / math.pi) * (x + 0.044715 * (x * x * x))))


def _outproj_kernel(x_ref, g_ref, ys5_ref, u_ref, d_ref, gw_ref, gb_ref, yf_ref, yb_ref, xs_ref, z_ref, dssd_ref,
                    nw_ref, ymla_ref, ygqa_ref, w_ref, o_ref):
    wd = ys5_ref.shape[1]
    gl = _gelu_tanh(ys5_ref[...] + d_ref[...] * u_ref[...])
    s5 = gl * _sigmoid(_dot(gl.astype(BF16), gw_ref[...]) + gb_ref[...])
    ssd = _rms((yf_ref[...] + yb_ref[...] + dssd_ref[...] * xs_ref[...]) * _silu(z_ref[...]), nw_ref[...])
    acc = _dot(s5.astype(BF16), w_ref[0:wd, :])
    acc += _dot(ssd.astype(BF16), w_ref[wd:2 * wd, :])
    acc += _dot(ymla_ref[...], w_ref[2 * wd:3 * wd, :])
    acc += _dot(ygqa_ref[...], w_ref[3 * wd:4 * wd, :])
    o_ref[...] = x_ref[...] + g_ref[...] * acc


def out_projection(x, nblk, row_of_block, mod4, ys5, p, u_blk, z_blk, d_s5, glu_w, glu_b, yf, yb, xbc, d_ssd, nw_ssd,
                   ymla, ygqa, w_out):
    t, d = x.shape
    wd = ys5.shape[1]
    row = lambda n: pl.BlockSpec((TM, n), lambda i: (i, 0))
    one = lambda n: pl.BlockSpec((1, n), lambda i: (0, 0))
    vec = lambda v: v.reshape(1, wd).astype(F32)
    return pl.pallas_call(
        _outproj_kernel,
        out_shape=jax.ShapeDtypeStruct((t, d), F32),
        grid=(nblk,),
        in_specs=[row(d),
                  pl.BlockSpec((None, None, 1, d), lambda i: (row_of_block(i), 5, 0, 0)),
                  row(wd),
                  pl.BlockSpec((TM, wd), lambda i: (i, u_blk)),
                  one(wd),
                  pl.BlockSpec((wd, wd), lambda i: (0, 0)),
                  one(wd),
                  row(wd), row(wd), row(wd),
                  pl.BlockSpec((TM, wd), lambda i: (i, z_blk)),
                  one(wd), one(wd),
                  row(wd), row(wd),
                  pl.BlockSpec((d, d), lambda i: (0, 0))],
        out_specs=row(d),
        compiler_params=_cp(("parallel",), 48),
        name="out_projection",
    )(x, mod4, ys5, p, vec(d_s5), glu_w.astype(BF16), vec(glu_b), yf, yb, xbc, p,
      vec(jnp.repeat(d_ssd.astype(F32), wd // d_ssd.shape[0])), vec(nw_ssd), ymla, ygqa, w_out)


COL_XBC, COL_GQA, COL_S5, COL_Z, COL_QL, COL_KVL, COL_KR, COL_DT = 0, 1024, 2048, 2560, 3072, 3456, 3584, 3712
N_IN = 3840


def _reorder_w_in(w_in, w_s5, w_ssd, xbc_w, n_dt, ql, kvl, rope, gqa_w):
    d = w_in.shape[0]
    o_ssd = w_s5
    o_mla = o_ssd + w_ssd + xbc_w + n_dt
    o_gqa = o_mla + ql + kvl + rope
    z = lambda n: jnp.zeros((d, n), w_in.dtype)
    parts = [w_in[:, o_ssd + w_ssd:o_ssd + w_ssd + xbc_w],
             w_in[:, o_gqa:o_gqa + gqa_w],
             w_in[:, 0:w_s5],
             w_in[:, o_ssd:o_ssd + w_ssd],
             w_in[:, o_mla:o_mla + ql],
             w_in[:, o_mla + ql:o_mla + ql + kvl],
             w_in[:, o_mla + ql + kvl:o_mla + ql + kvl + rope], z(LANE - rope),
             w_in[:, o_ssd + w_ssd + xbc_w:o_ssd + w_ssd + xbc_w + n_dt], z(LANE - n_dt)]
    out = jnp.concatenate(parts, axis=1)
    assert out.shape[1] == N_IN
    return out.astype(BF16)


def kernel(x, c, ctx, c_ctx, w_ada, b_ada, norm_w, ffn_w1, ffn_w3, ffn_w2, w_in, w_out, s5_lam_re, s5_lam_im, s5_log_dt, s5_b_re, s5_b_im, s5_c_re, s5_c_im, s5_d, s5_glu_w, s5_glu_b, ssd_conv_w, ssd_conv_b, ssd_dt_bias, ssd_a_log, ssd_d, ssd_norm_w, mla_q_norm_w, mla_w_qb, mla_kv_norm_w, mla_w_kvb, gqa_q_norm_w, gqa_k_norm_w, norm_f):
    nb, l_lat, d = x.shape
    l_ctx = ctx.shape[1]
    depth = w_ada.shape[0]
    nl, nc = nb * l_lat, nb * l_ctx
    t = nl + nc
    assert l_ctx == TP and nc == TM and l_lat % TM == 0 and l_lat % GRID_W == 0 and nb < SUBLANE
    assert SUBLANE % nb == 0 and (l_ctx // S5_CHUNK * nb) % SUBLANE == 0 and (l_lat // S5_CHUNK * nb) % SUBLANE == 0
    w_s5 = s5_d.shape[1]
    w_ssd = ssd_norm_w.shape[1]
    xbc_w = ssd_conv_w.shape[2]
    nheads = ssd_a_log.shape[2]
    ql, kvl = mla_q_norm_w.shape[1], mla_kv_norm_w.shape[1]
    gqa_hd = gqa_q_norm_w.shape[1]
    gqa_w = (GQA_HEADS + 2 * GQA_KV_HEADS) * gqa_hd
    assert (w_s5, w_ssd, xbc_w, ql, kvl, gqa_w) == (512, 512, 1024, 384, 128, 1024)

    xs = jnp.concatenate([x.reshape(nl, d), ctx.reshape(nc, d)], axis=0)
    act_in = jnp.concatenate([c, c_ctx[None, :], jnp.zeros((SUBLANE - nb - 1, d), F32)], axis=0)
    mod = ada_modulation(act_in, w_ada, b_ada)
    blocks_per_batch = l_lat // TM
    n_lat_blk = nl // TM
    row_of_block = lambda i: jnp.where(i < n_lat_blk, i // blocks_per_batch, nb)
    cos_m, sin_m = rope_tables(l_lat, MLA_ROPE, nb, l_ctx)
    cos_g, sin_g = rope_tables(l_lat, gqa_hd, nb, l_ctx)
    lat_tp, seq_tp = l_lat // TP, (l_ctx + l_lat) // TP
    seq_blk = lambda i: jnp.where(i < nl // TP, (i // lat_tp) * seq_tp + 1 + i % lat_tp, (i - nl // TP) * seq_tp)

    for l in range(depth):
        need_ctx = l < depth - 1
        mod4 = mod[l].reshape(SUBLANE, N_MOD, 1, d)
        nblk_all = t // TM
        nblk_tail = nblk_all if need_ctx else n_lat_blk
        w1, w3, w2 = ffn_w1[l].astype(BF16), ffn_w3[l].astype(BF16), ffn_w2[l].astype(BF16)
        xs = half_ffn(xs, nblk_all, row_of_block, mod4, 0, norm_w[l, 0], w1[0], w3[0], w2[0])
        w_in_r = _reorder_w_in(w_in[l], w_s5, w_ssd, xbc_w, 2 * nheads, ql, kvl, MLA_ROPE, gqa_w)
        p = in_projection(xs, nblk_all, row_of_block, mod4, norm_w[l, 1], w_in_r)

        tables = s5_tables(s5_lam_re[l], s5_lam_im[l], s5_log_dt[l], s5_b_re[l], s5_b_im[l], s5_c_re[l], s5_c_im[l])
        ys5 = s5_scan(p, nl, nb, l_lat, l_ctx, COL_S5 // 512, tables)

        xbc = ssd_conv(p, COL_XBC // xbc_w, ssd_conv_w[l], ssd_conv_b[l], l_lat, nl)
        dtt = p[:, COL_DT:COL_DT + 2 * nheads].T
        yf, yb = ssd_scan(xbc, p, COL_DT // LANE, dtt, ssd_dt_bias[l], ssd_a_log[l], nb, l_lat, l_ctx, nl)

        qm, km, vtm = mla_prep(p, COL_QL // ql, COL_KVL // kvl, COL_KR // LANE, mla_q_norm_w[l], mla_kv_norm_w[l],
                               mla_w_qb[l], mla_w_kvb[l], cos_m, sin_m, seq_blk)
        ymla = attention(qm, km, vtm, nb, l_lat, l_ctx, nl, MLA_HEADS, 1, 2 * LANE,
                         mla_w_kvb.shape[2] // MLA_HEADS - MLA_NOPE, need_ctx)
        qg, kg, vtg = gqa_prep(p, COL_GQA // gqa_w, gqa_q_norm_w[l], gqa_k_norm_w[l], cos_g, sin_g, seq_blk)
        ygqa = attention(qg, kg, vtg, nb, l_lat, l_ctx, nl, GQA_KV_HEADS, GQA_HEADS // GQA_KV_HEADS, gqa_hd, gqa_hd, need_ctx)

        xs = out_projection(xs, nblk_tail, row_of_block, mod4, ys5, p, COL_S5 // 512, COL_Z // 512, s5_d[l], s5_glu_w[l],
                            s5_glu_b[l], yf, yb, xbc, ssd_d[l], ssd_norm_w[l], ymla, ygqa, w_out[l].astype(BF16))
        xs = half_ffn(xs, nblk_tail, row_of_block, mod4, 6, norm_w[l, 2], w1[1], w3[1], w2[1],
                      final_w=None if need_ctx else norm_f)
    return xs.reshape(nb, l_lat, d)
```

```python
import functools
import math

import jax
import jax.numpy as jnp
from jax import lax
from jax.experimental import pallas as pl
from jax.experimental.pallas import tpu as pltpu

F32 = jnp.float32
BF16 = jnp.bfloat16
HIGHEST = lax.Precision.HIGHEST

EPS = 1e-6
ROPE_THETA = 10000.0
GRID_W = 64
N_MOD = 9
S5_GROUP = 16
S5_STATE = 64
SSD_HEADDIM = 64
SSD_GROUPS = 2
SSD_STATE = 128
SSD_CHUNK = 128
MLA_HEADS = 4
MLA_NOPE = 128
MLA_ROPE = 64
GQA_HEADS = 4
GQA_KV_HEADS = 2

LANE = 128
SUBLANE = 8
TM = 512
TP = 256
S5_CHUNK = 16
VT_PAD = 16
MIB = 1024 * 1024
LOG2E = 1.0 / math.log(2.0)
KEY_CHUNK_MAX = 1792
M_INIT = -1e30


def _cp(sem, vmem_mib):
    return pltpu.CompilerParams(dimension_semantics=sem, vmem_limit_bytes=vmem_mib * MIB)


def _sigmoid(x):
    return 1.0 / (1.0 + jnp.exp(-x))


def _silu(x):
    return x * _sigmoid(x)


def _rms(x, w):
    return x * lax.rsqrt(jnp.mean(x * x, axis=-1, keepdims=True) + EPS) * w


def _dot(a, b):
    return jnp.dot(a, b, preferred_element_type=F32)


def _dot_nt(a, b):
    return lax.dot_general(a, b, (((1,), (1,)), ((), ())), preferred_element_type=F32)


def _dot_tn(a, b):
    return lax.dot_general(a, b, (((0,), (0,)), ((), ())), preferred_element_type=F32)


def _ada_kernel(a_ref, w_ref, b_ref, o_ref):
    act = _silu(a_ref[...]).astype(BF16)
    o_ref[...] = _dot(act, w_ref[...].astype(BF16)) + b_ref[...]


def ada_modulation(act_in, w_ada, b_ada):
    depth, d, n = w_ada.shape
    tn = 1024
    return pl.pallas_call(
        _ada_kernel,
        out_shape=jax.ShapeDtypeStruct((depth, SUBLANE, n), F32),
        grid=(depth, n // tn),
        in_specs=[pl.BlockSpec((SUBLANE, d), lambda l, j: (0, 0)),
                  pl.BlockSpec((None, d, tn), lambda l, j: (l, 0, j)),
                  pl.BlockSpec((None, 1, tn), lambda l, j: (l, 0, j))],
        out_specs=pl.BlockSpec((None, SUBLANE, tn), lambda l, j: (l, 0, j)),
        compiler_params=_cp(("parallel", "arbitrary"), 40),
        name="ada_modulation",
    )(act_in, w_ada, b_ada.reshape(depth, 1, n))


def _ffn_kernel(x_ref, nw_ref, sh_ref, sc_ref, g_ref, w1_ref, w3_ref, w2_ref, *rest, final):
    fw_ref, o_ref, h_scr = rest if final else (None,) + rest
    j = pl.program_id(1)

    @pl.when(j == 0)
    def _():
        x = x_ref[...]
        h = _rms(x, nw_ref[...]) * (1.0 + sc_ref[...]) + sh_ref[...]
        h_scr[...] = h.astype(BF16)
        o_ref[...] = jnp.zeros_like(o_ref)

    h = h_scr[...]
    a = _dot(h, w1_ref[...])
    b = _dot(h, w3_ref[...])
    u = (_silu(a) * b).astype(BF16)
    o_ref[...] += _dot(u, w2_ref[...])

    @pl.when(j == pl.num_programs(1) - 1)
    def _():
        y = x_ref[...] + 0.5 * g_ref[...] * o_ref[...]
        o_ref[...] = _rms(y, fw_ref[...]) if final else y


def half_ffn(x, nblk, row_of_block, mod4, k0, norm_w, w1, w3, w2, layer, which, final_w=None):
    t, d = x.shape
    f = w1.shape[-1]
    tf = 512
    final = final_w is not None
    mspec = lambda k: pl.BlockSpec((None, None, 1, d), lambda i, j: (row_of_block(i), k, 0, 0))
    return pl.pallas_call(
        functools.partial(_ffn_kernel, final=final),
        out_shape=jax.ShapeDtypeStruct((nblk * TM if final else t, d), F32),
        grid=(nblk, f // tf),
        in_specs=[pl.BlockSpec((TM, d), lambda i, j: (i, 0)),
                  pl.BlockSpec((1, d), lambda i, j: (0, 0)),
                  mspec(k0), mspec(k0 + 1), mspec(k0 + 2),
                  pl.BlockSpec((None, None, d, tf), lambda i, j: (layer, which, 0, j)),
                  pl.BlockSpec((None, None, d, tf), lambda i, j: (layer, which, 0, j)),
                  pl.BlockSpec((None, None, tf, d), lambda i, j: (layer, which, j, 0))] + [pl.BlockSpec((1, d), lambda i, j: (0, 0))] * final,
        out_specs=pl.BlockSpec((TM, d), lambda i, j: (i, 0)),
        scratch_shapes=[pltpu.VMEM((TM, d), BF16)],
        compiler_params=_cp(("parallel", "arbitrary"), 48),
        name="half_ffn",
    )(x, norm_w.reshape(1, d), mod4, mod4, mod4, w1, w3, w2, *([final_w.reshape(1, d).astype(F32)] if final else []))


def _inproj_kernel(x_ref, nw_ref, sh_ref, sc_ref, w_ref, o_ref, h_scr):
    @pl.when(pl.program_id(1) == 0)
    def _():
        h = _rms(x_ref[...], nw_ref[...]) * (1.0 + sc_ref[...]) + sh_ref[...]
        h_scr[...] = h.astype(BF16)

    o_ref[...] = _dot(h_scr[...], w_ref[...])


def in_projection(x, nblk, row_of_block, mod4, norm_w, w):
    t, d = x.shape
    n = w.shape[1]
    tn = 1280
    mspec = lambda k: pl.BlockSpec((None, None, 1, d), lambda i, j: (row_of_block(i), k, 0, 0))
    return pl.pallas_call(
        _inproj_kernel,
        out_shape=jax.ShapeDtypeStruct((t, n), F32),
        grid=(nblk, n // tn),
        in_specs=[pl.BlockSpec((TM, d), lambda i, j: (i, 0)),
                  pl.BlockSpec((1, d), lambda i, j: (0, 0)),
                  mspec(3), mspec(4),
                  pl.BlockSpec((d, tn), lambda i, j: (0, j))],
        out_specs=pl.BlockSpec((TM, tn), lambda i, j: (i, j)),
        scratch_shapes=[pltpu.VMEM((TM, d), BF16)],
        compiler_params=_cp(("parallel", "arbitrary"), 40),
        name="in_projection",
    )(x, norm_w.reshape(1, d), mod4, mod4, w)


def s5_tables(lam_re, lam_im, log_dt, b_re, b_im, c_re, c_im):
    n = S5_CHUNK
    hp = functools.partial(jnp.einsum, precision=HIGHEST)
    step = jnp.exp(log_dt.astype(F32))[..., None]
    lr, li = lam_re.astype(F32), lam_im.astype(F32)
    jj = jnp.arange(n + 1, dtype=F32)
    mag = jnp.exp(lr[..., None] * step[..., None] * jj)
    ang = li[..., None] * step[..., None] * jj
    pr, pi = mag * jnp.cos(ang), mag * jnp.sin(ang)
    ar, ai = pr[..., 1], pi[..., 1]
    den = lr * lr + li * li
    qr = ((ar - 1.0) * lr + ai * li) / den
    qi = (ai * lr - (ar - 1.0) * li) / den
    br, bi = b_re.astype(F32), b_im.astype(F32)
    bbr = qr[..., None] * br - qi[..., None] * bi
    bbi = qr[..., None] * bi + qi[..., None] * br
    cr, ci = c_re.astype(F32), c_im.astype(F32)
    k = (hp('dgop,dgpj,dgpi->dgjoi', cr, pr[..., :n], bbr) - hp('dgop,dgpj,dgpi->dgjoi', cr, pi[..., :n], bbi)
         - hp('dgop,dgpj,dgpi->dgjoi', ci, pr[..., :n], bbi) - hp('dgop,dgpj,dgpi->dgjoi', ci, pi[..., :n], bbr))
    g = k.shape[1]
    h = k.shape[-1]
    s_idx = jnp.arange(n)[:, None]
    r_idx = jnp.arange(n)[None, :]
    lag = r_idx - s_idx
    kf = jnp.where((lag >= 0)[None, :, :, None, None], k[0][:, jnp.clip(lag, 0, n - 1)], 0.0)
    kb = jnp.where((lag <= 0)[None, :, :, None, None], k[1][:, jnp.clip(-lag, 0, n - 1)], 0.0)
    toep = jnp.transpose(kf + kb, (0, 1, 4, 2, 3)).reshape(g, n * h, n * h)

    def bend(d, pw_r, pw_i):
        re = hp('gps,gpi->gsip', pw_r, bbr[d]) - hp('gps,gpi->gsip', pw_i, bbi[d])
        im = hp('gps,gpi->gsip', pw_r, bbi[d]) + hp('gps,gpi->gsip', pw_i, bbr[d])
        return re.reshape(g, n * h, -1), im.reshape(g, n * h, -1)

    bf_r, bf_i = bend(0, pr[0][..., n - 1::-1][..., :n], pi[0][..., n - 1::-1][..., :n])
    bb_r, bb_i = bend(1, pr[1][..., :n], pi[1][..., :n])

    def coff(d, pw_r, pw_i):
        re = hp('gop,gpr->gpro', cr[d], pw_r) - hp('gop,gpr->gpro', ci[d], pw_i)
        im = -(hp('gop,gpr->gpro', cr[d], pw_i) + hp('gop,gpr->gpro', ci[d], pw_r))
        return re.reshape(g, -1, n * h), im.reshape(g, -1, n * h)

    cf_r, cf_i = coff(0, pr[0][..., 1:], pi[0][..., 1:])
    cb_r, cb_i = coff(1, pr[1][..., n:0:-1], pi[1][..., n:0:-1])

    q = g // 2
    w = n * h
    p = pr.shape[2]

    def pair_diag(m):
        m = m.reshape(q, 2, m.shape[1], m.shape[2])
        z = jnp.zeros_like(m[:, 0])
        return jnp.concatenate([jnp.concatenate([m[:, 0], z], axis=2), jnp.concatenate([z, m[:, 1]], axis=2)], axis=1)

    def step_major(m, axis):
        shp = m.shape
        m = m.reshape(shp[:axis] + (2, n, h) + shp[axis + 1:])
        return jnp.swapaxes(m, axis, axis + 1).reshape(shp)

    toep_p = step_major(step_major(pair_diag(toep), 1), 2)
    bend_p = jnp.concatenate([pair_diag(bf_r), pair_diag(bf_i), pair_diag(bb_r), pair_diag(bb_i)], axis=2)
    bend_p = step_major(bend_p, 1)
    coff_p = jnp.concatenate([pair_diag(cf_r), pair_diag(cf_i), pair_diag(cb_r), pair_diag(cb_i)], axis=1)
    coff_p = step_major(coff_p, 2)
    a16 = jnp.stack([pr[0][..., n], pi[0][..., n], pr[1][..., n], pi[1][..., n]], axis=1)
    a16 = a16.reshape(q, 2, 4, p).transpose(0, 2, 1, 3).reshape(q, 4, 2 * p)
    a16 = jnp.concatenate([a16, jnp.zeros((q, 4, 2 * p), F32)], axis=1)
    return toep_p.astype(BF16), bend_p.astype(BF16), coff_p.astype(BF16), a16


def _s5_local_kernel(u_ref, toep_ref, bend_ref, y_ref, h_ref):
    u = u_ref[...]
    y_ref[...] = _dot(u, toep_ref[...])
    h_ref[...] = _dot(u, bend_ref[...])


def _s5_carry_kernel(a_ref, hl_ref, hs_ref, *, nb, n_ctx_tiles, n_lat_tiles):
    a = a_ref[...]
    afr, afi, abr, abi = a[0:1], a[1:2], a[2:3], a[3:4]
    spt = SUBLANE // nb

    def tile_pass(tile_f, tile_b, carry):
        fr, fi, br, bi = carry
        rf = pl.multiple_of(tile_f * SUBLANE, SUBLANE)
        rb = pl.multiple_of(tile_b * SUBLANE, SUBLANE)
        lf = hl_ref[pl.ds(rf, SUBLANE), 0:2 * LANE]
        lb = hl_ref[pl.ds(rb, SUBLANE), 2 * LANE:4 * LANE]
        of_r, of_i, ob_r, ob_i = [], [], [], []
        for k in range(spt):
            of_r.append(fr)
            of_i.append(fi)
            lo = k * nb
            nr = afr * fr - afi * fi + lf[lo:lo + nb, 0:LANE]
            ni = afr * fi + afi * fr + lf[lo:lo + nb, LANE:2 * LANE]
            fr, fi = nr, ni
        for k in range(spt - 1, -1, -1):
            ob_r.append(br)
            ob_i.append(bi)
            lo = k * nb
            nr = abr * br - abi * bi + lb[lo:lo + nb, 0:LANE]
            ni = abr * bi + abi * br + lb[lo:lo + nb, LANE:2 * LANE]
            br, bi = nr, ni
        hs_ref[pl.ds(rf, SUBLANE), 0:LANE] = jnp.concatenate(of_r, axis=0)
        hs_ref[pl.ds(rf, SUBLANE), LANE:2 * LANE] = jnp.concatenate(of_i, axis=0)
        hs_ref[pl.ds(rb, SUBLANE), 2 * LANE:3 * LANE] = jnp.concatenate(ob_r[::-1], axis=0)
        hs_ref[pl.ds(rb, SUBLANE), 3 * LANE:4 * LANE] = jnp.concatenate(ob_i[::-1], axis=0)
        return fr, fi, br, bi

    z = jnp.zeros((nb, LANE), F32)
    carry = (z, z, z, z)
    carry = lax.fori_loop(0, n_ctx_tiles, lambda t, c: tile_pass(t, n_ctx_tiles - 1 - t, c), carry)
    lax.fori_loop(0, n_lat_tiles,
                  lambda t, c: tile_pass(n_ctx_tiles + t, n_ctx_tiles + n_lat_tiles - 1 - t, c), carry)


def _s5_out_kernel(yi_ref, hs_ref, coff_ref, y_ref):
    y_ref[...] = (yi_ref[...] + _dot(hs_ref[...].astype(BF16), coff_ref[...])).astype(y_ref.dtype)


def s5_scan(p, nl, nb, l_lat, l_ctx, col_blk, tables):
    toep, bend, coff, a16 = tables
    q = toep.shape[0]
    w = 2 * S5_CHUNK * S5_GROUP
    t = p.shape[0]
    u = p[:, col_blk * w:(col_blk + 1) * w].astype(BF16)
    c_lat, c_ctx = l_lat // S5_CHUNK, l_ctx // S5_CHUNK
    wp = 2 * S5_GROUP

    def to_chunks(v, nchunk):
        v = v.reshape(nb, nchunk, S5_CHUNK, q, wp)
        return jnp.transpose(v, (3, 1, 0, 2, 4)).reshape(q, nchunk * nb, w)

    def from_chunks(v, nchunk):
        v = v.reshape(q, nchunk, nb, S5_CHUNK, wp)
        return jnp.transpose(v, (2, 1, 3, 0, 4)).reshape(nb * nchunk * S5_CHUNK, w)

    uc = jnp.concatenate([to_chunks(u[nl:], c_ctx), to_chunks(u[:nl], c_lat)], axis=1)
    nrow = uc.shape[1]
    yi, hl = pl.pallas_call(
        _s5_local_kernel,
        out_shape=(jax.ShapeDtypeStruct((q, nrow, w), F32), jax.ShapeDtypeStruct((nrow, q * w), F32)),
        grid=(q,),
        in_specs=[pl.BlockSpec((None, nrow, w), lambda g: (g, 0, 0)),
                  pl.BlockSpec((None, w, w), lambda g: (g, 0, 0)),
                  pl.BlockSpec((None, w, w), lambda g: (g, 0, 0))],
        out_specs=(pl.BlockSpec((None, nrow, w), lambda g: (g, 0, 0)),
                   pl.BlockSpec((nrow, w), lambda g: (0, g))),
        compiler_params=_cp(("parallel",), 40),
        name="s5_local",
    )(uc, toep, bend)
    hs = pl.pallas_call(
        functools.partial(_s5_carry_kernel, nb=nb, n_ctx_tiles=c_ctx * nb // SUBLANE,
                          n_lat_tiles=c_lat * nb // SUBLANE),
        out_shape=jax.ShapeDtypeStruct((nrow, q * w), F32),
        grid=(q,),
        in_specs=[pl.BlockSpec((None, SUBLANE, LANE), lambda g: (g, 0, 0)),
                  pl.BlockSpec((nrow, w), lambda g: (0, g))],
        out_specs=pl.BlockSpec((nrow, w), lambda g: (0, g)),
        compiler_params=_cp(("parallel",), 40),
        name="s5_carry",
    )(a16, hl)
    y = pl.pallas_call(
        _s5_out_kernel,
        out_shape=jax.ShapeDtypeStruct((q, nrow, w), BF16),
        grid=(q,),
        in_specs=[pl.BlockSpec((None, nrow, w), lambda g: (g, 0, 0)),
                  pl.BlockSpec((nrow, w), lambda g: (0, g)),
                  pl.BlockSpec((None, w, w), lambda g: (g, 0, 0))],
        out_specs=pl.BlockSpec((None, nrow, w), lambda g: (g, 0, 0)),
        compiler_params=_cp(("parallel",), 40),
        name="s5_out",
    )(yi, hs, coff)
    n_ctx_rows = c_ctx * nb
    return jnp.concatenate([from_chunks(y[:, n_ctx_rows:], c_lat), from_chunks(y[:, :n_ctx_rows], c_ctx)], axis=0)


def _ssd_conv_kernel(x_ref, prev_ref, next_ref, w_ref, b_ref, o_ref, pad_scr, *, blocks_per_seq, n_lat_blocks):
    i = pl.program_id(0)
    is_lat = i < n_lat_blocks
    first = jnp.logical_or(jnp.logical_not(is_lat), i % blocks_per_seq == 0)
    last = jnp.logical_or(jnp.logical_not(is_lat), (i + 1) % blocks_per_seq == 0)
    tp = x_ref.shape[0]
    pad_scr[0:SUBLANE, :] = jnp.where(first, 0.0, prev_ref[...])
    pad_scr[SUBLANE:SUBLANE + tp, :] = x_ref[...]
    pad_scr[SUBLANE + tp:2 * SUBLANE + tp, :] = jnp.where(last, 0.0, next_ref[...])
    w = w_ref[...]
    kk = 5
    acc = b_ref[...] + w[0:1] * pad_scr[SUBLANE - kk // 2:SUBLANE - kk // 2 + tp, :]
    for k in range(1, kk):
        off = SUBLANE - kk // 2 + k
        acc = acc + w[k:k + 1] * pad_scr[off:off + tp, :]
    o_ref[...] = _silu(acc)


def ssd_conv(p, col_blk, conv_w, conv_b, l_lat, nl):
    t = p.shape[0]
    c = conv_w.shape[1]
    assert conv_w.shape[0] == 5
    r = TP // SUBLANE
    wpad = jnp.concatenate([conv_w.astype(F32), jnp.zeros((SUBLANE - conv_w.shape[0], c), F32)], axis=0)
    nsub = t // SUBLANE
    return pl.pallas_call(
        functools.partial(_ssd_conv_kernel, blocks_per_seq=l_lat // TP, n_lat_blocks=nl // TP),
        out_shape=jax.ShapeDtypeStruct((t, c), F32),
        grid=(t // TP,),
        in_specs=[pl.BlockSpec((TP, c), lambda i: (i, col_blk)),
                  pl.BlockSpec((SUBLANE, c), lambda i: (jnp.maximum(i * r - 1, 0), col_blk)),
                  pl.BlockSpec((SUBLANE, c), lambda i: (jnp.minimum((i + 1) * r, nsub - 1), col_blk)),
                  pl.BlockSpec((SUBLANE, c), lambda i: (0, 0)),
                  pl.BlockSpec((1, c), lambda i: (0, 0))],
        out_specs=pl.BlockSpec((TP, c), lambda i: (i, 0)),
        scratch_shapes=[pltpu.VMEM((TP + 2 * SUBLANE, c), F32)],
        compiler_params=_cp(("parallel",), 32),
        name="ssd_conv",
    )(p, p, p, wpad, conv_b.reshape(1, c).astype(F32))


def _softplus(x):
    return jnp.maximum(x, 0.0) + jnp.log(1.0 + jnp.exp(-jnp.abs(x)))


def _split_bf16(x, terms):
    out = []
    for _ in range(terms):
        piece = x.astype(BF16)
        out.append(piece)
        x = x - piece.astype(F32)
    return out


def _ssd_chunk(xbc_ref, dt_ref, dtt_ref, bias, biast, a_row, a_col, e, st_ref, *, reverse, lane0, nheads, hd, ns,
               ngroups):
    tc = xbc_ref.shape[0]
    wx = nheads * hd
    hpg = nheads // ngroups
    x = xbc_ref[:, 0:wx]
    rows = lax.broadcasted_iota(jnp.int32, (tc, tc), 0)
    cols = lax.broadcasted_iota(jnp.int32, (tc, tc), 1)
    keep = (cols >= rows) if reverse else (cols <= rows)
    tri = keep.astype(BF16)
    tri_t = ((rows >= cols) if reverse else (rows <= cols)).astype(BF16)
    dt = _softplus(dt_ref[...] + bias)
    acs = sum(_dot(tri, piece) for piece in _split_bf16(dt * a_row, 3))
    dtt = _softplus(dtt_ref[...] + biast)
    acst = sum(_dot(piece, tri_t) for piece in _split_bf16(dtt * a_col, 3))
    edge = tc - 1 if not reverse else 0
    tot = acs[edge:edge + 1, :]
    fac = jnp.concatenate([dt, jnp.exp(acs), jnp.exp(tot - acs)], axis=0)
    spread = sum(_dot(piece, e) for piece in _split_bf16(fac, 2))
    dtx, eacs, edec = spread[0:tc], spread[tc:2 * tc], spread[2 * tc:3 * tc]
    chunk_decay = eacs[edge:edge + 1, :]
    xd = x * dtx
    xdb = xd.astype(BF16)
    xdd = (xd * edec).astype(BF16)
    ys = []
    for g in range(ngroups):
        bg = xbc_ref[:, wx + g * ns:wx + (g + 1) * ns].astype(BF16)
        cg = xbc_ref[:, wx + (ngroups + g) * ns:wx + (ngroups + g + 1) * ns].astype(BF16)
        cb = _dot_nt(cg, bg)
        lo, hi = g * hpg * hd, (g + 1) * hpg * hd
        st = st_ref[:, lo:hi]
        y_off = _dot(cg, st.astype(BF16)) * eacs[:, lo:hi]
        st_ref[:, lo:hi] = st * chunk_decay[:, lo:hi] + _dot_tn(bg, xdd[:, lo:hi])
        yd = []
        for hh in range(hpg):
            h = g * hpg + hh
            col = acs[:, lane0 + h:lane0 + h + 1]
            row = acst[lane0 + h:lane0 + h + 1, :]
            lm = jnp.exp(jnp.where(keep, col - row, -jnp.inf))
            yd.append(_dot((cb * lm).astype(BF16), xdb[:, h * hd:(h + 1) * hd]))
        ys.append(jnp.concatenate(yd, axis=1) + y_off)
    return jnp.concatenate(ys, axis=1)


def _ssd_scan_kernel(xf_ref, dtf_ref, dttf_ref, xb_ref, dtb_ref, dttb_ref, bias_ref, biast_ref, a_ref, at_ref,
                     ef_ref, eb_ref, yf_ref, yb_ref, st_scr, *, nheads, hd, ns, ngroups):
    @pl.when(pl.program_id(1) == 0)
    def _():
        st_scr[...] = jnp.zeros_like(st_scr)

    kw = dict(nheads=nheads, hd=hd, ns=ns, ngroups=ngroups)
    bias, biast, a_row, a_col = bias_ref[...], biast_ref[...], a_ref[...], at_ref[...]
    yf_ref[...] = _ssd_chunk(xf_ref, dtf_ref, dttf_ref, bias, biast, a_row, a_col, ef_ref[...], st_scr.at[0],
                             reverse=False, lane0=0, **kw)
    yb_ref[...] = _ssd_chunk(xb_ref, dtb_ref, dttb_ref, bias, biast, a_row, a_col, eb_ref[...], st_scr.at[1],
                             reverse=True, lane0=nheads, **kw)


def ssd_scan(xbc, p, dt_blk, dtt, dt_bias, a_log, nb, l_lat, l_ctx, nl):
    t = xbc.shape[0]
    nheads = a_log.shape[1]
    hd = SSD_HEADDIM
    wx = nheads * hd
    tc = SSD_CHUNK
    nch_ctx, nch_lat = l_ctx // tc, l_lat // tc
    nch = nch_ctx + nch_lat
    a = -jnp.exp(a_log.astype(F32))
    a_row = jnp.zeros((1, LANE), F32).at[0, :2 * nheads].set(a.reshape(-1))
    bias_row = jnp.zeros((1, LANE), F32).at[0, :2 * nheads].set(dt_bias.astype(F32).reshape(-1))
    a_col = a.reshape(2 * nheads, 1)
    bias_col = dt_bias.astype(F32).reshape(2 * nheads, 1)

    def expand(direction):
        hidx = jnp.arange(LANE)[:, None] - direction * nheads
        return (hidx == (jnp.arange(wx)[None, :] // hd)).astype(BF16)

    def chunk_block(reverse):
        def f(b, k):
            is_ctx = k < nch_ctx
            ctx_c = (nch_ctx - 1 - k) if reverse else k
            lat_c = (nch_lat - 1 - (k - nch_ctx)) if reverse else (k - nch_ctx)
            return jnp.where(is_ctx, (nl + b * l_ctx) // tc + ctx_c, b * nch_lat + lat_c)
        return f

    blk_f, blk_b = chunk_block(False), chunk_block(True)
    chunk_specs = lambda blk: [pl.BlockSpec((tc, xbc.shape[1]), lambda b, k: (blk(b, k), 0)),
                               pl.BlockSpec((tc, LANE), lambda b, k: (blk(b, k), dt_blk)),
                               pl.BlockSpec((2 * nheads, tc), lambda b, k: (0, blk(b, k)))]
    const = lambda shape: pl.BlockSpec(shape, lambda b, k: (0, 0))
    return pl.pallas_call(
        functools.partial(_ssd_scan_kernel, nheads=nheads, hd=hd, ns=SSD_STATE, ngroups=SSD_GROUPS),
        out_shape=(jax.ShapeDtypeStruct((t, wx), F32), jax.ShapeDtypeStruct((t, wx), F32)),
        grid=(nb, nch),
        in_specs=chunk_specs(blk_f) + chunk_specs(blk_b) + [const((1, LANE)), const((2 * nheads, 1)), const((1, LANE)),
                                                            const((2 * nheads, 1)), const((LANE, wx)), const((LANE, wx))],
        out_specs=(pl.BlockSpec((tc, wx), lambda b, k: (blk_f(b, k), 0)),
                   pl.BlockSpec((tc, wx), lambda b, k: (blk_b(b, k), 0))),
        scratch_shapes=[pltpu.VMEM((2, SSD_STATE, wx), F32)],
        compiler_params=_cp(("parallel", "arbitrary"), 32),
        name="ssd_scan",
    )(xbc, p, dtt, xbc, p, dtt, bias_row, bias_col, a_row, a_col, expand(0), expand(1))


def rope_tables(l_lat, rot_dim, nb, l_ctx):
    rows = l_lat // GRID_W
    row = jnp.repeat(jnp.arange(rows, dtype=F32), GRID_W)
    col = jnp.tile(jnp.arange(GRID_W, dtype=F32), rows)
    n_freq = rot_dim // 4
    inv_freq = ROPE_THETA ** (-jnp.arange(n_freq, dtype=F32) / n_freq)
    ang = jnp.concatenate([row[:, None] * inv_freq, col[:, None] * inv_freq], axis=-1)
    cos = jnp.concatenate([jnp.tile(jnp.cos(ang), (nb, 1)), jnp.ones((nb * l_ctx, rot_dim // 2), F32)], axis=0)
    sin = jnp.concatenate([jnp.tile(jnp.sin(ang), (nb, 1)), jnp.zeros((nb * l_ctx, rot_dim // 2), F32)], axis=0)
    return cos, sin


def _store_values_t(vt_ref, vt, nheads):
    dv = vt.shape[0] // nheads
    tail = (lax.broadcasted_iota(jnp.int32, (VT_PAD, vt.shape[1]), 0) == 0).astype(BF16)
    for h in range(nheads):
        vt_ref[h * (dv + VT_PAD):h * (dv + VT_PAD) + dv, :] = vt[h * dv:(h + 1) * dv].astype(BF16)
        vt_ref[h * (dv + VT_PAD) + dv:(h + 1) * (dv + VT_PAD), :] = tail


def _mla_prep_kernel(ql_ref, kvl_ref, kr_ref, qnw_ref, kvnw_ref, wq_ref, wk_ref, wvt_ref, c_ref, s1_ref, s2_ref,
                     q_ref, k_ref, vt_ref, *, nheads, scale):
    qn = _rms(ql_ref[...], qnw_ref[...]).astype(BF16)
    q = _dot(qn, wq_ref[...])
    ckv = _rms(kvl_ref[...], kvnw_ref[...]).astype(BF16)
    kn = _dot(ckv, wk_ref[...])
    _store_values_t(vt_ref, _dot_nt(wvt_ref[...], ckv), nheads)
    c, s1, s2 = c_ref[...], s1_ref[...], s2_ref[...]

    def rope(v):
        return v * c + pltpu.roll(v, 3 * LANE // 4, 1) * s1 + pltpu.roll(v, LANE // 4, 1) * s2

    kr = rope(kr_ref[...]).astype(BF16)
    for h in range(nheads):
        b0 = 2 * LANE * h
        q_ref[:, b0:b0 + LANE] = (q[:, b0:b0 + LANE] * scale).astype(BF16)
        q_ref[:, b0 + LANE:b0 + 2 * LANE] = (rope(q[:, b0 + LANE:b0 + 2 * LANE]) * scale).astype(BF16)
        k_ref[:, b0:b0 + LANE] = kn[:, LANE * h:LANE * (h + 1)].astype(BF16)
        k_ref[:, b0 + LANE:b0 + 2 * LANE] = kr


def mla_prep(p, blk_q, blk_kv, blk_kr, q_norm_w, kv_norm_w, w_qb, w_kvb, cos, sin, seq_blk):
    t = p.shape[0]
    nh = MLA_HEADS
    qk = MLA_NOPE + MLA_ROPE
    ql, kvl = q_norm_w.shape[0], kv_norm_w.shape[0]
    assert MLA_NOPE == LANE and MLA_ROPE == LANE // 2 and kvl == LANE
    dv = w_kvb.shape[1] // nh - MLA_NOPE
    wq = w_qb.reshape(ql, nh, qk)
    wq = jnp.concatenate([wq, jnp.zeros((ql, nh, 2 * LANE - qk), w_qb.dtype)], axis=-1).reshape(ql, nh * 2 * LANE)
    wkv = w_kvb.reshape(kvl, nh, MLA_NOPE + dv)
    wk = wkv[:, :, :MLA_NOPE].reshape(kvl, nh * MLA_NOPE)
    wvt = wkv[:, :, MLA_NOPE:].reshape(kvl, nh * dv).T
    half = MLA_ROPE // 2
    zeros = jnp.zeros((t, half), F32)
    c_tab = jnp.concatenate([cos, cos, jnp.ones((t, 2 * half), F32)], axis=1)
    s1_tab = jnp.concatenate([-sin, zeros, zeros, zeros], axis=1)
    s2_tab = jnp.concatenate([zeros, sin, zeros, zeros], axis=1)
    row = lambda w: pl.BlockSpec((TP, w), lambda i: (i, 0))
    full = lambda a: pl.BlockSpec(a.shape, lambda i: (0, 0))
    wq, wk, wvt = wq.astype(BF16), wk.astype(BF16), wvt.astype(BF16)
    qnw, kvnw = q_norm_w.reshape(1, ql).astype(F32), kv_norm_w.reshape(1, kvl).astype(F32)
    return pl.pallas_call(
        functools.partial(_mla_prep_kernel, nheads=nh, scale=float(qk) ** -0.5 * LOG2E),
        out_shape=(jax.ShapeDtypeStruct((t, nh * 2 * LANE), BF16), jax.ShapeDtypeStruct((t, nh * 2 * LANE), BF16),
                   jax.ShapeDtypeStruct((nh * (dv + VT_PAD), t), BF16)),
        grid=(t // TP,),
        in_specs=[pl.BlockSpec((TP, ql), lambda i: (i, blk_q)),
                  pl.BlockSpec((TP, kvl), lambda i: (i, blk_kv)),
                  pl.BlockSpec((TP, LANE), lambda i: (i, blk_kr)),
                  full(qnw), full(kvnw), full(wq), full(wk), full(wvt), row(LANE), row(LANE), row(LANE)],
        out_specs=(row(nh * 2 * LANE), pl.BlockSpec((TP, nh * 2 * LANE), lambda i: (seq_blk(i), 0)),
                   pl.BlockSpec((nh * (dv + VT_PAD), TP), lambda i: (0, seq_blk(i)))),
        compiler_params=_cp(("parallel",), 32),
        name="mla_prep",
    )(p, p, p, qnw, kvnw, wq, wk, wvt, c_tab, s1_tab, s2_tab)


def _gqa_prep_kernel(pg_ref, qnw_ref, knw_ref, c_ref, s_ref, q_ref, k_ref, vt_ref, *, nq, nkv, scale):
    c, s = c_ref[...], s_ref[...]

    def norm_rope(v, w):
        n = _rms(v, w)
        return n * c + pltpu.roll(n, LANE // 2, 1) * s

    for h in range(nq):
        q_ref[:, h * LANE:(h + 1) * LANE] = (norm_rope(pg_ref[:, h * LANE:(h + 1) * LANE], qnw_ref[...]) * scale).astype(BF16)
    for h in range(nkv):
        o = (nq + h) * LANE
        k_ref[:, h * LANE:(h + 1) * LANE] = norm_rope(pg_ref[:, o:o + LANE], knw_ref[...]).astype(BF16)
    o = (nq + nkv) * LANE
    _store_values_t(vt_ref, pg_ref[:, o:o + nkv * LANE].T, nkv)


def gqa_prep(p, blk, q_norm_w, k_norm_w, cos, sin, seq_blk):
    t = p.shape[0]
    nq, nkv = GQA_HEADS, GQA_KV_HEADS
    hd = q_norm_w.shape[0]
    assert hd == LANE
    w = (nq + 2 * nkv) * hd
    c_tab = jnp.concatenate([cos, cos], axis=1)
    s_tab = jnp.concatenate([-sin, sin], axis=1)
    row = lambda n: pl.BlockSpec((TP, n), lambda i: (i, 0))
    one = pl.BlockSpec((1, hd), lambda i: (0, 0))
    return pl.pallas_call(
        functools.partial(_gqa_prep_kernel, nq=nq, nkv=nkv, scale=float(hd) ** -0.5 * LOG2E),
        out_shape=(jax.ShapeDtypeStruct((t, nq * hd), BF16), jax.ShapeDtypeStruct((t, nkv * hd), BF16),
                   jax.ShapeDtypeStruct((nkv * (hd + VT_PAD), t), BF16)),
        grid=(t // TP,),
        in_specs=[pl.BlockSpec((TP, w), lambda i: (i, blk)), one, one, row(hd), row(hd)],
        out_specs=(row(nq * hd), pl.BlockSpec((TP, nkv * hd), lambda i: (seq_blk(i), 0)),
                   pl.BlockSpec((nkv * (hd + VT_PAD), TP), lambda i: (0, seq_blk(i)))),
        compiler_params=_cp(("parallel",), 32),
        name="gqa_prep",
    )(p, q_norm_w.reshape(1, hd).astype(F32), k_norm_w.reshape(1, hd).astype(F32), c_tab, s_tab)


def _flash_kernel(*refs, rep, rows, dk, dv, tk, n_chunks, aliased):
    if aliased:
        refs = refs[1:]
    if n_chunks > 1:
        q_ref, k_ref, vt_ref, o_ref, m_scr, acc_scr, s_scr = refs
    else:
        q_ref, k_ref, vt_ref, o_ref, m_scr, acc_scr = refs
    tq = q_ref.shape[0] // rows
    streams = [(h, r) for h in range(rows) for r in range(rep)]
    qs = [q_ref[h * tq:(h + 1) * tq, r * dk:(r + 1) * dk] for h, r in streams]

    def chunk_softmax(st, vt):
        mc = jnp.max(st, axis=0, keepdims=True)
        pt = jnp.exp2(st - mc).astype(BF16)
        return mc, _dot(vt, pt)

    def merge(s, mc, pv):
        m_old = m_scr[s]
        m_new = jnp.maximum(m_old, mc)
        m_scr[s] = m_new
        acc_scr[s] = jnp.exp2(m_old - m_new) * acc_scr[s] + jnp.exp2(mc - m_new) * pv

    m_scr[...] = jnp.full(m_scr.shape, M_INIT, F32)
    acc_scr[...] = jnp.zeros_like(acc_scr)
    if n_chunks == 1:
        for s, q in enumerate(qs):
            merge(s, *chunk_softmax(_dot_nt(k_ref[...], q), vt_ref[...]))
    else:
        for s, q in enumerate(qs):
            s_scr[s, 0] = _dot_nt(k_ref[0:tk, :], q)

        def step(j, cur, last=False):
            start = (lambda c: c * tk) if isinstance(j, int) else (lambda c: pl.multiple_of(c * tk, tk))
            off = start(j)
            for s, q in enumerate(qs):
                if not last:
                    s_scr[s, 1 - cur] = _dot_nt(k_ref[pl.ds(start(j + 1), tk), :], q)
                merge(s, *chunk_softmax(s_scr[s, cur], vt_ref[:, pl.ds(off, tk)]))

        def body(i, carry):
            step(2 * i, 0)
            step(2 * i + 1, 1)
            return carry

        lax.fori_loop(0, n_chunks // 2 - 1, body, 0)
        step(n_chunks - 2, 0)
        step(n_chunks - 1, 1, last=True)
    for s, (h, r) in enumerate(streams):
        out = acc_scr[s, 0:dv, :] / acc_scr[s, dv:dv + 1, :]
        o_ref[h * tq:(h + 1) * tq, r * dv:(r + 1) * dv] = out.T.astype(o_ref.dtype)


def _key_chunk(s_len):
    for tk in range(KEY_CHUNK_MAX, 0, -LANE):
        if s_len % tk == 0 and (s_len // tk) % 2 == 0:
            return tk
    raise ValueError(f"no key chunk for sequence length {s_len}")


def attention(q, k, vt, nb, l_lat, l_ctx, nl, hkv, rep, dk, dv, need_ctx):
    t = q.shape[0]
    s_len = l_ctx + l_lat
    tq = 512 if l_lat % 512 == 0 else TP
    tk = _key_chunk(s_len)
    rows = 2 // rep if l_lat % (tq * 2 // rep) == 0 else 1
    nq = l_lat // (tq * rows)
    cb = nl // l_ctx
    spb = s_len // l_ctx
    dvx = dv + VT_PAD
    scratch = lambda ns, n: [pltpu.VMEM((ns, 1, n), F32), pltpu.VMEM((ns, dvx, n), F32)]
    o = pl.pallas_call(
        functools.partial(_flash_kernel, rep=rep, rows=rows, dk=dk, dv=dv, tk=tk, n_chunks=s_len // tk, aliased=False),
        out_shape=jax.ShapeDtypeStruct((t, hkv * rep * dv), BF16),
        grid=(nb, hkv, nq),
        in_specs=[pl.BlockSpec((tq * rows, rep * dk), lambda b, g, i: (b * nq + i, g)),
                  pl.BlockSpec((s_len, dk), lambda b, g, i: (b, g)),
                  pl.BlockSpec((dvx, s_len), lambda b, g, i: (g, b))],
        out_specs=pl.BlockSpec((tq * rows, rep * dv), lambda b, g, i: (b * nq + i, g)),
        scratch_shapes=scratch(rows * rep, tq) + [pltpu.VMEM((rows * rep, 2, tk, tq), F32)],
        compiler_params=_cp(("parallel", "parallel", "arbitrary"), 56),
        name="attention_latent",
    )(q, k, vt)
    if not need_ctx:
        return o
    return pl.pallas_call(
        functools.partial(_flash_kernel, rep=rep, rows=1, dk=dk, dv=dv, tk=l_ctx, n_chunks=1, aliased=True),
        out_shape=jax.ShapeDtypeStruct((t, hkv * rep * dv), BF16),
        grid=(nb, hkv),
        in_specs=[pl.BlockSpec(memory_space=pl.ANY),
                  pl.BlockSpec((l_ctx, rep * dk), lambda b, g: (cb + b, g)),
                  pl.BlockSpec((l_ctx, dk), lambda b, g: (b * spb, g)),
                  pl.BlockSpec((dvx, l_ctx), lambda b, g: (g, b * spb))],
        out_specs=pl.BlockSpec((l_ctx, rep * dv), lambda b, g: (cb + b, g)),
        scratch_shapes=scratch(rep, l_ctx),
        input_output_aliases={0: 0},
        compiler_params=_cp(("parallel", "parallel"), 32),
        name="attention_context",
    )(o, q, k, vt)


def _gelu_tanh(x):
    return 0.5 * x * (1.0 + jnp.tanh(math.sqrt(2.0 / math.pi) * (x + 0.044715 * (x * x * x))))


def _outproj_kernel(x_ref, g_ref, ys5_ref, u_ref, d_ref, gw_ref, gb_ref, yf_ref, yb_ref, xs_ref, z_ref, dssd_ref,
                    nw_ref, ymla_ref, ygqa_ref, w_ref, o_ref):
    wd = ys5_ref.shape[1]
    gl = _gelu_tanh(ys5_ref[...].astype(F32) + d_ref[...] * u_ref[...])
    s5 = gl * _sigmoid(_dot(gl.astype(BF16), gw_ref[...]) + gb_ref[...])
    ssd = _rms((yf_ref[...] + yb_ref[...] + dssd_ref[...] * xs_ref[...]) * _silu(z_ref[...]), nw_ref[...])
    acc = _dot(s5.astype(BF16), w_ref[0:wd, :])
    acc += _dot(ssd.astype(BF16), w_ref[wd:2 * wd, :])
    acc += _dot(ymla_ref[...], w_ref[2 * wd:3 * wd, :])
    acc += _dot(ygqa_ref[...], w_ref[3 * wd:4 * wd, :])
    o_ref[...] = x_ref[...] + g_ref[...] * acc


def out_projection(x, nblk, row_of_block, mod4, ys5, p, u_blk, z_blk, d_s5, glu_w, glu_b, yf, yb, xbc, d_ssd, nw_ssd,
                   ymla, ygqa, w_out, layer):
    t, d = x.shape
    wd = ys5.shape[1]
    row = lambda n: pl.BlockSpec((TM, n), lambda i: (i, 0))
    one = lambda n: pl.BlockSpec((1, n), lambda i: (0, 0))
    vec = lambda v: v.reshape(1, wd).astype(F32)
    return pl.pallas_call(
        _outproj_kernel,
        out_shape=jax.ShapeDtypeStruct((t, d), F32),
        grid=(nblk,),
        in_specs=[row(d),
                  pl.BlockSpec((None, None, 1, d), lambda i: (row_of_block(i), 5, 0, 0)),
                  row(wd),
                  pl.BlockSpec((TM, wd), lambda i: (i, u_blk)),
                  one(wd),
                  pl.BlockSpec((wd, wd), lambda i: (0, 0)),
                  one(wd),
                  row(wd), row(wd), row(wd),
                  pl.BlockSpec((TM, wd), lambda i: (i, z_blk)),
                  one(wd), one(wd),
                  row(wd), row(wd),
                  pl.BlockSpec((None, d, d), lambda i: (layer, 0, 0))],
        out_specs=row(d),
        compiler_params=_cp(("parallel",), 48),
        name="out_projection",
    )(x, mod4, ys5, p, vec(d_s5), glu_w.astype(BF16), vec(glu_b), yf, yb, xbc, p,
      vec(jnp.repeat(d_ssd.astype(F32), wd // d_ssd.shape[0])), vec(nw_ssd), ymla, ygqa, w_out)


COL_XBC, COL_GQA, COL_S5, COL_Z, COL_QL, COL_KVL, COL_KR, COL_DT = 0, 1024, 2048, 2560, 3072, 3456, 3584, 3712
N_IN = 3840


def _reorder_w_in(w_in, w_s5, w_ssd, xbc_w, n_dt, ql, kvl, rope, gqa_w):
    d = w_in.shape[0]
    o_ssd = w_s5
    o_mla = o_ssd + w_ssd + xbc_w + n_dt
    o_gqa = o_mla + ql + kvl + rope
    z = lambda n: jnp.zeros((d, n), w_in.dtype)
    parts = [w_in[:, o_ssd + w_ssd:o_ssd + w_ssd + xbc_w],
             w_in[:, o_gqa:o_gqa + gqa_w],
             w_in[:, 0:w_s5],
             w_in[:, o_ssd:o_ssd + w_ssd],
             w_in[:, o_mla:o_mla + ql],
             w_in[:, o_mla + ql:o_mla + ql + kvl],
             w_in[:, o_mla + ql + kvl:o_mla + ql + kvl + rope], z(LANE - rope),
             w_in[:, o_ssd + w_ssd + xbc_w:o_ssd + w_ssd + xbc_w + n_dt], z(LANE - n_dt)]
    out = jnp.concatenate(parts, axis=1)
    assert out.shape[1] == N_IN
    return out.astype(BF16)


def kernel(x, c, ctx, c_ctx, w_ada, b_ada, norm_w, ffn_w1, ffn_w3, ffn_w2, w_in, w_out, s5_lam_re, s5_lam_im, s5_log_dt, s5_b_re, s5_b_im, s5_c_re, s5_c_im, s5_d, s5_glu_w, s5_glu_b, ssd_conv_w, ssd_conv_b, ssd_dt_bias, ssd_a_log, ssd_d, ssd_norm_w, mla_q_norm_w, mla_w_qb, mla_kv_norm_w, mla_w_kvb, gqa_q_norm_w, gqa_k_norm_w, norm_f):
    nb, l_lat, d = x.shape
    l_ctx = ctx.shape[1]
    depth = w_ada.shape[0]
    nl, nc = nb * l_lat, nb * l_ctx
    t = nl + nc
    assert l_ctx == TP and nc == TM and l_lat % TM == 0 and l_lat % GRID_W == 0 and nb < SUBLANE
    assert SUBLANE % nb == 0 and (l_ctx // S5_CHUNK * nb) % SUBLANE == 0 and (l_lat // S5_CHUNK * nb) % SUBLANE == 0
    w_s5 = s5_d.shape[1]
    w_ssd = ssd_norm_w.shape[1]
    xbc_w = ssd_conv_w.shape[2]
    nheads = ssd_a_log.shape[2]
    ql, kvl = mla_q_norm_w.shape[1], mla_kv_norm_w.shape[1]
    gqa_hd = gqa_q_norm_w.shape[1]
    gqa_w = (GQA_HEADS + 2 * GQA_KV_HEADS) * gqa_hd
    assert (w_s5, w_ssd, xbc_w, ql, kvl, gqa_w) == (512, 512, 1024, 384, 128, 1024)

    xs = jnp.concatenate([x.reshape(nl, d), ctx.reshape(nc, d)], axis=0)
    act_in = jnp.concatenate([c, c_ctx[None, :], jnp.zeros((SUBLANE - nb - 1, d), F32)], axis=0)
    mod = ada_modulation(act_in, w_ada, b_ada)
    blocks_per_batch = l_lat // TM
    n_lat_blk = nl // TM
    row_of_block = lambda i: jnp.where(i < n_lat_blk, i // blocks_per_batch, nb)
    cos_m, sin_m = rope_tables(l_lat, MLA_ROPE, nb, l_ctx)
    cos_g, sin_g = rope_tables(l_lat, gqa_hd, nb, l_ctx)
    lat_tp, seq_tp = l_lat // TP, (l_ctx + l_lat) // TP
    seq_blk = lambda i: jnp.where(i < nl // TP, (i // lat_tp) * seq_tp + 1 + i % lat_tp, (i - nl // TP) * seq_tp)

    w1, w3, w2, w_out_b = ffn_w1.astype(BF16), ffn_w3.astype(BF16), ffn_w2.astype(BF16), w_out.astype(BF16)
    for l in range(depth):
        need_ctx = l < depth - 1
        mod4 = mod[l].reshape(SUBLANE, N_MOD, 1, d)
        nblk_all = t // TM
        nblk_tail = nblk_all if need_ctx else n_lat_blk
        xs = half_ffn(xs, nblk_all, row_of_block, mod4, 0, norm_w[l, 0], w1, w3, w2, l, 0)
        w_in_r = _reorder_w_in(w_in[l], w_s5, w_ssd, xbc_w, 2 * nheads, ql, kvl, MLA_ROPE, gqa_w)
        p = in_projection(xs, nblk_all, row_of_block, mod4, norm_w[l, 1], w_in_r)

        tables = s5_tables(s5_lam_re[l], s5_lam_im[l], s5_log_dt[l], s5_b_re[l], s5_b_im[l], s5_c_re[l], s5_c_im[l])
        ys5 = s5_scan(p, nl, nb, l_lat, l_ctx, COL_S5 // 512, tables)

        xbc = ssd_conv(p, COL_XBC // xbc_w, ssd_conv_w[l], ssd_conv_b[l], l_lat, nl)
        dtt = p[:, COL_DT:COL_DT + 2 * nheads].T
        yf, yb = ssd_scan(xbc, p, COL_DT // LANE, dtt, ssd_dt_bias[l], ssd_a_log[l], nb, l_lat, l_ctx, nl)

        qm, km, vtm = mla_prep(p, COL_QL // ql, COL_KVL // kvl, COL_KR // LANE, mla_q_norm_w[l], mla_kv_norm_w[l],
                               mla_w_qb[l], mla_w_kvb[l], cos_m, sin_m, seq_blk)
        ymla = attention(qm, km, vtm, nb, l_lat, l_ctx, nl, MLA_HEADS, 1, 2 * LANE,
                         mla_w_kvb.shape[2] // MLA_HEADS - MLA_NOPE, need_ctx)
        qg, kg, vtg = gqa_prep(p, COL_GQA // gqa_w, gqa_q_norm_w[l], gqa_k_norm_w[l], cos_g, sin_g, seq_blk)
        ygqa = attention(qg, kg, vtg, nb, l_lat, l_ctx, nl, GQA_KV_HEADS, GQA_HEADS // GQA_KV_HEADS, gqa_hd, gqa_hd, need_ctx)

        xs = out_projection(xs, nblk_tail, row_of_block, mod4, ys5, p, COL_S5 // 512, COL_Z // 512, s5_d[l], s5_glu_w[l],
                            s5_glu_b[l], yf, yb, xbc, ssd_d[l], ssd_norm_w[l], ymla, ygqa, w_out_b, l)
        xs = half_ffn(xs, nblk_tail, row_of_block, mod4, 6, norm_w[l, 2], w1, w3, w2, l, 1,
                      final_w=None if need_ctx else norm_f)
    return xs.reshape(nb, l_lat, d)
```

```python
import functools
import math

import jax
import jax.numpy as jnp
from jax import lax
from jax.experimental import pallas as pl
from jax.experimental.pallas import tpu as pltpu

F32 = jnp.float32
BF16 = jnp.bfloat16
HIGHEST = lax.Precision.HIGHEST

EPS = 1e-6
ROPE_THETA = 10000.0
GRID_W = 64
N_MOD = 9
S5_GROUP = 16
S5_STATE = 64
SSD_HEADDIM = 64
SSD_GROUPS = 2
SSD_STATE = 128
SSD_CHUNK = 128
MLA_HEADS = 4
MLA_NOPE = 128
MLA_ROPE = 64
GQA_HEADS = 4
GQA_KV_HEADS = 2

LANE = 128
SUBLANE = 8
TM = 512
TP = 256
S5_CHUNK = 16
VT_PAD = 16
MIB = 1024 * 1024
LOG2E = 1.0 / math.log(2.0)
KEY_CHUNK_MAX = 1792
M_INIT = -1e30


def _cp(sem, vmem_mib):
    return pltpu.CompilerParams(dimension_semantics=sem, vmem_limit_bytes=vmem_mib * MIB)


def _sigmoid(x):
    return 1.0 / (1.0 + jnp.exp(-x))


def _silu(x):
    return x * _sigmoid(x)


def _rms(x, w):
    return x * lax.rsqrt(jnp.mean(x * x, axis=-1, keepdims=True) + EPS) * w


def _dot(a, b):
    return jnp.dot(a, b, preferred_element_type=F32)


def _dot_nt(a, b):
    return lax.dot_general(a, b, (((1,), (1,)), ((), ())), preferred_element_type=F32)


def _dot_tn(a, b):
    return lax.dot_general(a, b, (((0,), (0,)), ((), ())), preferred_element_type=F32)


def _ada_kernel(a_ref, w_ref, b_ref, o_ref):
    act = _silu(a_ref[...]).astype(BF16)
    o_ref[...] = _dot(act, w_ref[...].astype(BF16)) + b_ref[...]


def ada_modulation(act_in, w_ada, b_ada):
    depth, d, n = w_ada.shape
    tn = 1024
    return pl.pallas_call(
        _ada_kernel,
        out_shape=jax.ShapeDtypeStruct((depth, SUBLANE, n), F32),
        grid=(depth, n // tn),
        in_specs=[pl.BlockSpec((SUBLANE, d), lambda l, j: (0, 0)),
                  pl.BlockSpec((None, d, tn), lambda l, j: (l, 0, j)),
                  pl.BlockSpec((None, 1, tn), lambda l, j: (l, 0, j))],
        out_specs=pl.BlockSpec((None, SUBLANE, tn), lambda l, j: (l, 0, j)),
        compiler_params=_cp(("parallel", "arbitrary"), 40),
        name="ada_modulation",
    )(act_in, w_ada, b_ada.reshape(depth, 1, n))


def _ffn_kernel(x_ref, nw_ref, sh_ref, sc_ref, g_ref, w1_ref, w3_ref, w2_ref, *rest, final):
    fw_ref, o_ref, h_scr = rest if final else (None,) + rest
    j = pl.program_id(1)

    def swiglu_chunk(h):
        a = _dot(h, w1_ref[...])
        b = _dot(h, w3_ref[...])
        return _dot((_silu(a) * b).astype(BF16), w2_ref[...])

    @pl.when(j == 0)
    def _():
        half = x_ref.shape[0] // 2
        for r in range(2):
            rows = slice(r * half, (r + 1) * half)
            h = (_rms(x_ref[rows, :], nw_ref[...]) * (1.0 + sc_ref[...]) + sh_ref[...]).astype(BF16)
            h_scr[rows, :] = h
            o_ref[rows, :] = swiglu_chunk(h)

    @pl.when(j > 0)
    def _():
        o_ref[...] += swiglu_chunk(h_scr[...])

    @pl.when(j == pl.num_programs(1) - 1)
    def _():
        y = x_ref[...] + 0.5 * g_ref[...] * o_ref[...]
        o_ref[...] = _rms(y, fw_ref[...]) if final else y


def half_ffn(x, nblk, row_of_block, mod4, k0, norm_w, w1, w3, w2, layer, which, final_w=None):
    t, d = x.shape
    f = w1.shape[-1]
    tf = 512
    final = final_w is not None
    mspec = lambda k: pl.BlockSpec((None, None, 1, d), lambda i, j: (row_of_block(i), k, 0, 0))
    return pl.pallas_call(
        functools.partial(_ffn_kernel, final=final),
        out_shape=jax.ShapeDtypeStruct((nblk * TM if final else t, d), F32),
        grid=(nblk, f // tf),
        in_specs=[pl.BlockSpec((TM, d), lambda i, j: (i, 0)),
                  pl.BlockSpec((1, d), lambda i, j: (0, 0)),
                  mspec(k0), mspec(k0 + 1), mspec(k0 + 2),
                  pl.BlockSpec((None, None, d, tf), lambda i, j: (layer, which, 0, j)),
                  pl.BlockSpec((None, None, d, tf), lambda i, j: (layer, which, 0, j)),
                  pl.BlockSpec((None, None, tf, d), lambda i, j: (layer, which, j, 0))] + [pl.BlockSpec((1, d), lambda i, j: (0, 0))] * final,
        out_specs=pl.BlockSpec((TM, d), lambda i, j: (i, 0)),
        scratch_shapes=[pltpu.VMEM((TM, d), BF16)],
        compiler_params=_cp(("parallel", "arbitrary"), 48),
        name="half_ffn",
    )(x, norm_w.reshape(1, d), mod4, mod4, mod4, w1, w3, w2, *([final_w.reshape(1, d).astype(F32)] if final else []))


def _inproj_kernel(x_ref, nw_ref, sh_ref, sc_ref, w_ref, o_ref, h_scr):
    @pl.when(pl.program_id(1) == 0)
    def _():
        half = x_ref.shape[0] // 2
        for r in range(2):
            rows = slice(r * half, (r + 1) * half)
            h = (_rms(x_ref[rows, :], nw_ref[...]) * (1.0 + sc_ref[...]) + sh_ref[...]).astype(BF16)
            h_scr[rows, :] = h
            o_ref[rows, :] = _dot(h, w_ref[...])

    @pl.when(pl.program_id(1) > 0)
    def _():
        o_ref[...] = _dot(h_scr[...], w_ref[...])


def in_projection(x, nblk, row_of_block, mod4, norm_w, w):
    t, d = x.shape
    n = w.shape[1]
    tn = 1280
    mspec = lambda k: pl.BlockSpec((None, None, 1, d), lambda i, j: (row_of_block(i), k, 0, 0))
    return pl.pallas_call(
        _inproj_kernel,
        out_shape=jax.ShapeDtypeStruct((t, n), F32),
        grid=(nblk, n // tn),
        in_specs=[pl.BlockSpec((TM, d), lambda i, j: (i, 0)),
                  pl.BlockSpec((1, d), lambda i, j: (0, 0)),
                  mspec(3), mspec(4),
                  pl.BlockSpec((d, tn), lambda i, j: (0, j))],
        out_specs=pl.BlockSpec((TM, tn), lambda i, j: (i, j)),
        scratch_shapes=[pltpu.VMEM((TM, d), BF16)],
        compiler_params=_cp(("parallel", "arbitrary"), 40),
        name="in_projection",
    )(x, norm_w.reshape(1, d), mod4, mod4, w)


def s5_tables(lam_re, lam_im, log_dt, b_re, b_im, c_re, c_im):
    n = S5_CHUNK
    hp = functools.partial(jnp.einsum, precision=HIGHEST)
    step = jnp.exp(log_dt.astype(F32))[..., None]
    lr, li = lam_re.astype(F32), lam_im.astype(F32)
    jj = jnp.arange(n + 1, dtype=F32)
    mag = jnp.exp(lr[..., None] * step[..., None] * jj)
    ang = li[..., None] * step[..., None] * jj
    pr, pi = mag * jnp.cos(ang), mag * jnp.sin(ang)
    ar, ai = pr[..., 1], pi[..., 1]
    den = lr * lr + li * li
    qr = ((ar - 1.0) * lr + ai * li) / den
    qi = (ai * lr - (ar - 1.0) * li) / den
    br, bi = b_re.astype(F32), b_im.astype(F32)
    bbr = qr[..., None] * br - qi[..., None] * bi
    bbi = qr[..., None] * bi + qi[..., None] * br
    cr, ci = c_re.astype(F32), c_im.astype(F32)
    k = (hp('dgop,dgpj,dgpi->dgjoi', cr, pr[..., :n], bbr) - hp('dgop,dgpj,dgpi->dgjoi', cr, pi[..., :n], bbi)
         - hp('dgop,dgpj,dgpi->dgjoi', ci, pr[..., :n], bbi) - hp('dgop,dgpj,dgpi->dgjoi', ci, pi[..., :n], bbr))
    g = k.shape[1]
    h = k.shape[-1]
    s_idx = jnp.arange(n)[:, None]
    r_idx = jnp.arange(n)[None, :]
    lag = r_idx - s_idx
    kf = jnp.where((lag >= 0)[None, :, :, None, None], k[0][:, jnp.clip(lag, 0, n - 1)], 0.0)
    kb = jnp.where((lag <= 0)[None, :, :, None, None], k[1][:, jnp.clip(-lag, 0, n - 1)], 0.0)
    toep = jnp.transpose(kf + kb, (0, 1, 4, 2, 3)).reshape(g, n * h, n * h)

    def bend(d, pw_r, pw_i):
        re = hp('gps,gpi->gsip', pw_r, bbr[d]) - hp('gps,gpi->gsip', pw_i, bbi[d])
        im = hp('gps,gpi->gsip', pw_r, bbi[d]) + hp('gps,gpi->gsip', pw_i, bbr[d])
        return re.reshape(g, n * h, -1), im.reshape(g, n * h, -1)

    bf_r, bf_i = bend(0, pr[0][..., n - 1::-1][..., :n], pi[0][..., n - 1::-1][..., :n])
    bb_r, bb_i = bend(1, pr[1][..., :n], pi[1][..., :n])

    def coff(d, pw_r, pw_i):
        re = hp('gop,gpr->gpro', cr[d], pw_r) - hp('gop,gpr->gpro', ci[d], pw_i)
        im = -(hp('gop,gpr->gpro', cr[d], pw_i) + hp('gop,gpr->gpro', ci[d], pw_r))
        return re.reshape(g, -1, n * h), im.reshape(g, -1, n * h)

    cf_r, cf_i = coff(0, pr[0][..., 1:], pi[0][..., 1:])
    cb_r, cb_i = coff(1, pr[1][..., n:0:-1], pi[1][..., n:0:-1])

    q = g // 2
    w = n * h
    p = pr.shape[2]

    def pair_diag(m):
        m = m.reshape(q, 2, m.shape[1], m.shape[2])
        z = jnp.zeros_like(m[:, 0])
        return jnp.concatenate([jnp.concatenate([m[:, 0], z], axis=2), jnp.concatenate([z, m[:, 1]], axis=2)], axis=1)

    def step_major(m, axis):
        shp = m.shape
        m = m.reshape(shp[:axis] + (2, n, h) + shp[axis + 1:])
        return jnp.swapaxes(m, axis, axis + 1).reshape(shp)

    toep_p = step_major(step_major(pair_diag(toep), 1), 2)
    bend_p = jnp.concatenate([pair_diag(bf_r), pair_diag(bf_i), pair_diag(bb_r), pair_diag(bb_i)], axis=2)
    bend_p = step_major(bend_p, 1)
    coff_p = jnp.concatenate([pair_diag(cf_r), pair_diag(cf_i), pair_diag(cb_r), pair_diag(cb_i)], axis=1)
    coff_p = step_major(coff_p, 2)
    a16 = jnp.stack([pr[0][..., n], pi[0][..., n], pr[1][..., n], pi[1][..., n]], axis=1)
    a16 = a16.reshape(q, 2, 4, p).transpose(0, 2, 1, 3).reshape(q, 4, 2 * p)
    a16 = jnp.concatenate([a16, jnp.zeros((q, 4, 2 * p), F32)], axis=1)
    return toep_p.astype(BF16), bend_p.astype(BF16), coff_p.astype(BF16), a16


def _s5_local_kernel(u_ref, toep_ref, bend_ref, y_ref, h_ref):
    u = u_ref[...]
    y_ref[...] = _dot(u, toep_ref[...])
    h_ref[...] = _dot(u, bend_ref[...])


def _s5_carry_kernel(a_ref, hl_ref, hs_ref, *, nb, n_ctx_tiles, n_lat_tiles):
    a = a_ref[...]
    afr, afi, abr, abi = a[0:1], a[1:2], a[2:3], a[3:4]
    spt = SUBLANE // nb

    def tile_pass(tile_f, tile_b, carry):
        fr, fi, br, bi = carry
        rf = pl.multiple_of(tile_f * SUBLANE, SUBLANE)
        rb = pl.multiple_of(tile_b * SUBLANE, SUBLANE)
        lf = hl_ref[pl.ds(rf, SUBLANE), 0:2 * LANE]
        lb = hl_ref[pl.ds(rb, SUBLANE), 2 * LANE:4 * LANE]
        of_r, of_i, ob_r, ob_i = [], [], [], []
        for k in range(spt):
            of_r.append(fr)
            of_i.append(fi)
            lo = k * nb
            nr = afr * fr - afi * fi + lf[lo:lo + nb, 0:LANE]
            ni = afr * fi + afi * fr + lf[lo:lo + nb, LANE:2 * LANE]
            fr, fi = nr, ni
        for k in range(spt - 1, -1, -1):
            ob_r.append(br)
            ob_i.append(bi)
            lo = k * nb
            nr = abr * br - abi * bi + lb[lo:lo + nb, 0:LANE]
            ni = abr * bi + abi * br + lb[lo:lo + nb, LANE:2 * LANE]
            br, bi = nr, ni
        hs_ref[pl.ds(rf, SUBLANE), 0:LANE] = jnp.concatenate(of_r, axis=0)
        hs_ref[pl.ds(rf, SUBLANE), LANE:2 * LANE] = jnp.concatenate(of_i, axis=0)
        hs_ref[pl.ds(rb, SUBLANE), 2 * LANE:3 * LANE] = jnp.concatenate(ob_r[::-1], axis=0)
        hs_ref[pl.ds(rb, SUBLANE), 3 * LANE:4 * LANE] = jnp.concatenate(ob_i[::-1], axis=0)
        return fr, fi, br, bi

    z = jnp.zeros((nb, LANE), F32)
    carry = (z, z, z, z)
    carry = lax.fori_loop(0, n_ctx_tiles, lambda t, c: tile_pass(t, n_ctx_tiles - 1 - t, c), carry)
    lax.fori_loop(0, n_lat_tiles,
                  lambda t, c: tile_pass(n_ctx_tiles + t, n_ctx_tiles + n_lat_tiles - 1 - t, c), carry)


def _s5_out_kernel(yi_ref, hs_ref, coff_ref, y_ref):
    y_ref[...] = (yi_ref[...] + _dot(hs_ref[...].astype(BF16), coff_ref[...])).astype(y_ref.dtype)


def s5_scan(p, nl, nb, l_lat, l_ctx, col_blk, tables):
    toep, bend, coff, a16 = tables
    q = toep.shape[0]
    w = 2 * S5_CHUNK * S5_GROUP
    t = p.shape[0]
    u = p[:, col_blk * w:(col_blk + 1) * w].astype(BF16)
    c_lat, c_ctx = l_lat // S5_CHUNK, l_ctx // S5_CHUNK
    wp = 2 * S5_GROUP

    def to_chunks(v, nchunk):
        v = v.reshape(nb, nchunk, S5_CHUNK, q, wp)
        return jnp.transpose(v, (3, 1, 0, 2, 4)).reshape(q, nchunk * nb, w)

    def from_chunks(v, nchunk):
        v = v.reshape(q, nchunk, nb, S5_CHUNK, wp)
        return jnp.transpose(v, (2, 1, 3, 0, 4)).reshape(nb * nchunk * S5_CHUNK, w)

    uc = jnp.concatenate([to_chunks(u[nl:], c_ctx), to_chunks(u[:nl], c_lat)], axis=1)
    nrow = uc.shape[1]
    yi, hl = pl.pallas_call(
        _s5_local_kernel,
        out_shape=(jax.ShapeDtypeStruct((q, nrow, w), F32), jax.ShapeDtypeStruct((nrow, q * w), F32)),
        grid=(q,),
        in_specs=[pl.BlockSpec((None, nrow, w), lambda g: (g, 0, 0)),
                  pl.BlockSpec((None, w, w), lambda g: (g, 0, 0)),
                  pl.BlockSpec((None, w, w), lambda g: (g, 0, 0))],
        out_specs=(pl.BlockSpec((None, nrow, w), lambda g: (g, 0, 0)),
                   pl.BlockSpec((nrow, w), lambda g: (0, g))),
        compiler_params=_cp(("parallel",), 40),
        name="s5_local",
    )(uc, toep, bend)
    hs = pl.pallas_call(
        functools.partial(_s5_carry_kernel, nb=nb, n_ctx_tiles=c_ctx * nb // SUBLANE,
                          n_lat_tiles=c_lat * nb // SUBLANE),
        out_shape=jax.ShapeDtypeStruct((nrow, q * w), F32),
        grid=(q,),
        in_specs=[pl.BlockSpec((None, SUBLANE, LANE), lambda g: (g, 0, 0)),
                  pl.BlockSpec((nrow, w), lambda g: (0, g))],
        out_specs=pl.BlockSpec((nrow, w), lambda g: (0, g)),
        compiler_params=_cp(("parallel",), 40),
        name="s5_carry",
    )(a16, hl)
    y = pl.pallas_call(
        _s5_out_kernel,
        out_shape=jax.ShapeDtypeStruct((q, nrow, w), BF16),
        grid=(q,),
        in_specs=[pl.BlockSpec((None, nrow, w), lambda g: (g, 0, 0)),
                  pl.BlockSpec((nrow, w), lambda g: (0, g)),
                  pl.BlockSpec((None, w, w), lambda g: (g, 0, 0))],
        out_specs=pl.BlockSpec((None, nrow, w), lambda g: (g, 0, 0)),
        compiler_params=_cp(("parallel",), 40),
        name="s5_out",
    )(yi, hs, coff)
    n_ctx_rows = c_ctx * nb
    return jnp.concatenate([from_chunks(y[:, n_ctx_rows:], c_lat), from_chunks(y[:, :n_ctx_rows], c_ctx)], axis=0)


def _ssd_conv_kernel(x_ref, prev_ref, next_ref, w_ref, b_ref, o_ref, pad_scr, *, blocks_per_seq, n_lat_blocks):
    i = pl.program_id(0)
    is_lat = i < n_lat_blocks
    first = jnp.logical_or(jnp.logical_not(is_lat), i % blocks_per_seq == 0)
    last = jnp.logical_or(jnp.logical_not(is_lat), (i + 1) % blocks_per_seq == 0)
    tp = x_ref.shape[0]
    pad_scr[0:SUBLANE, :] = jnp.where(first, 0.0, prev_ref[...])
    pad_scr[SUBLANE:SUBLANE + tp, :] = x_ref[...]
    pad_scr[SUBLANE + tp:2 * SUBLANE + tp, :] = jnp.where(last, 0.0, next_ref[...])
    w = w_ref[...]
    kk = 5
    acc = b_ref[...] + w[0:1] * pad_scr[SUBLANE - kk // 2:SUBLANE - kk // 2 + tp, :]
    for k in range(1, kk):
        off = SUBLANE - kk // 2 + k
        acc = acc + w[k:k + 1] * pad_scr[off:off + tp, :]
    o_ref[...] = _silu(acc)


def ssd_conv(p, col_blk, conv_w, conv_b, l_lat, nl):
    t = p.shape[0]
    c = conv_w.shape[1]
    assert conv_w.shape[0] == 5
    r = TP // SUBLANE
    wpad = jnp.concatenate([conv_w.astype(F32), jnp.zeros((SUBLANE - conv_w.shape[0], c), F32)], axis=0)
    nsub = t // SUBLANE
    return pl.pallas_call(
        functools.partial(_ssd_conv_kernel, blocks_per_seq=l_lat // TP, n_lat_blocks=nl // TP),
        out_shape=jax.ShapeDtypeStruct((t, c), F32),
        grid=(t // TP,),
        in_specs=[pl.BlockSpec((TP, c), lambda i: (i, col_blk)),
                  pl.BlockSpec((SUBLANE, c), lambda i: (jnp.maximum(i * r - 1, 0), col_blk)),
                  pl.BlockSpec((SUBLANE, c), lambda i: (jnp.minimum((i + 1) * r, nsub - 1), col_blk)),
                  pl.BlockSpec((SUBLANE, c), lambda i: (0, 0)),
                  pl.BlockSpec((1, c), lambda i: (0, 0))],
        out_specs=pl.BlockSpec((TP, c), lambda i: (i, 0)),
        scratch_shapes=[pltpu.VMEM((TP + 2 * SUBLANE, c), F32)],
        compiler_params=_cp(("parallel",), 32),
        name="ssd_conv",
    )(p, p, p, wpad, conv_b.reshape(1, c).astype(F32))


def _softplus(x):
    return jnp.maximum(x, 0.0) + jnp.log(1.0 + jnp.exp(-jnp.abs(x)))


def _split_bf16(x, terms):
    out = []
    for _ in range(terms):
        piece = x.astype(BF16)
        out.append(piece)
        x = x - piece.astype(F32)
    return out


def _ssd_chunk(xbc_ref, dt_ref, dtt_ref, bias, biast, a_row, a_col, e, st_ref, *, reverse, lane0, nheads, hd, ns,
               ngroups):
    tc = xbc_ref.shape[0]
    wx = nheads * hd
    hpg = nheads // ngroups
    x = xbc_ref[:, 0:wx]
    rows = lax.broadcasted_iota(jnp.int32, (tc, tc), 0)
    cols = lax.broadcasted_iota(jnp.int32, (tc, tc), 1)
    keep = (cols >= rows) if reverse else (cols <= rows)
    tri = keep.astype(BF16)
    tri_t = ((rows >= cols) if reverse else (rows <= cols)).astype(BF16)
    dt = _softplus(dt_ref[...] + bias)
    acs = sum(_dot(tri, piece) for piece in _split_bf16(dt * a_row, 3))
    dtt = _softplus(dtt_ref[...] + biast)
    acst = sum(_dot(piece, tri_t) for piece in _split_bf16(dtt * a_col, 3))
    edge = tc - 1 if not reverse else 0
    tot = acs[edge:edge + 1, :]
    fac = jnp.concatenate([dt, jnp.exp(acs), jnp.exp(tot - acs)], axis=0)
    spread = sum(_dot(piece, e) for piece in _split_bf16(fac, 2))
    dtx, eacs, edec = spread[0:tc], spread[tc:2 * tc], spread[2 * tc:3 * tc]
    chunk_decay = eacs[edge:edge + 1, :]
    xd = x * dtx
    xdb = xd.astype(BF16)
    xdd = (xd * edec).astype(BF16)
    ys = []
    for g in range(ngroups):
        bg = xbc_ref[:, wx + g * ns:wx + (g + 1) * ns].astype(BF16)
        cg = xbc_ref[:, wx + (ngroups + g) * ns:wx + (ngroups + g + 1) * ns].astype(BF16)
        cb = _dot_nt(cg, bg)
        lo, hi = g * hpg * hd, (g + 1) * hpg * hd
        st = st_ref[:, lo:hi]
        y_off = _dot(cg, st.astype(BF16)) * eacs[:, lo:hi]
        st_ref[:, lo:hi] = st * chunk_decay[:, lo:hi] + _dot_tn(bg, xdd[:, lo:hi])
        yd = []
        for hh in range(hpg):
            h = g * hpg + hh
            col = acs[:, lane0 + h:lane0 + h + 1]
            row = acst[lane0 + h:lane0 + h + 1, :]
            lm = jnp.exp(jnp.where(keep, col - row, -jnp.inf))
            yd.append(_dot((cb * lm).astype(BF16), xdb[:, h * hd:(h + 1) * hd]))
        ys.append(jnp.concatenate(yd, axis=1) + y_off)
    return jnp.concatenate(ys, axis=1)


def _ssd_scan_kernel(xf_ref, dtf_ref, dttf_ref, xb_ref, dtb_ref, dttb_ref, bias_ref, biast_ref, a_ref, at_ref,
                     ef_ref, eb_ref, yf_ref, yb_ref, st_scr, *, nheads, hd, ns, ngroups):
    @pl.when(pl.program_id(1) == 0)
    def _():
        st_scr[...] = jnp.zeros_like(st_scr)

    kw = dict(nheads=nheads, hd=hd, ns=ns, ngroups=ngroups)
    bias, biast, a_row, a_col = bias_ref[...], biast_ref[...], a_ref[...], at_ref[...]
    yf_ref[...] = _ssd_chunk(xf_ref, dtf_ref, dttf_ref, bias, biast, a_row, a_col, ef_ref[...], st_scr.at[0],
                             reverse=False, lane0=0, **kw)
    yb_ref[...] = _ssd_chunk(xb_ref, dtb_ref, dttb_ref, bias, biast, a_row, a_col, eb_ref[...], st_scr.at[1],
                             reverse=True, lane0=nheads, **kw)


def ssd_scan(xbc, p, dt_blk, dtt, dt_bias, a_log, nb, l_lat, l_ctx, nl):
    t = xbc.shape[0]
    nheads = a_log.shape[1]
    hd = SSD_HEADDIM
    wx = nheads * hd
    tc = SSD_CHUNK
    nch_ctx, nch_lat = l_ctx // tc, l_lat // tc
    nch = nch_ctx + nch_lat
    a = -jnp.exp(a_log.astype(F32))
    a_row = jnp.zeros((1, LANE), F32).at[0, :2 * nheads].set(a.reshape(-1))
    bias_row = jnp.zeros((1, LANE), F32).at[0, :2 * nheads].set(dt_bias.astype(F32).reshape(-1))
    a_col = a.reshape(2 * nheads, 1)
    bias_col = dt_bias.astype(F32).reshape(2 * nheads, 1)

    def expand(direction):
        hidx = jnp.arange(LANE)[:, None] - direction * nheads
        return (hidx == (jnp.arange(wx)[None, :] // hd)).astype(BF16)

    def chunk_block(reverse):
        def f(b, k):
            is_ctx = k < nch_ctx
            ctx_c = (nch_ctx - 1 - k) if reverse else k
            lat_c = (nch_lat - 1 - (k - nch_ctx)) if reverse else (k - nch_ctx)
            return jnp.where(is_ctx, (nl + b * l_ctx) // tc + ctx_c, b * nch_lat + lat_c)
        return f

    blk_f, blk_b = chunk_block(False), chunk_block(True)
    chunk_specs = lambda blk: [pl.BlockSpec((tc, xbc.shape[1]), lambda b, k: (blk(b, k), 0)),
                               pl.BlockSpec((tc, LANE), lambda b, k: (blk(b, k), dt_blk)),
                               pl.BlockSpec((2 * nheads, tc), lambda b, k: (0, blk(b, k)))]
    const = lambda shape: pl.BlockSpec(shape, lambda b, k: (0, 0))
    return pl.pallas_call(
        functools.partial(_ssd_scan_kernel, nheads=nheads, hd=hd, ns=SSD_STATE, ngroups=SSD_GROUPS),
        out_shape=(jax.ShapeDtypeStruct((t, wx), F32), jax.ShapeDtypeStruct((t, wx), F32)),
        grid=(nb, nch),
        in_specs=chunk_specs(blk_f) + chunk_specs(blk_b) + [const((1, LANE)), const((2 * nheads, 1)), const((1, LANE)),
                                                            const((2 * nheads, 1)), const((LANE, wx)), const((LANE, wx))],
        out_specs=(pl.BlockSpec((tc, wx), lambda b, k: (blk_f(b, k), 0)),
                   pl.BlockSpec((tc, wx), lambda b, k: (blk_b(b, k), 0))),
        scratch_shapes=[pltpu.VMEM((2, SSD_STATE, wx), F32)],
        compiler_params=_cp(("parallel", "arbitrary"), 32),
        name="ssd_scan",
    )(xbc, p, dtt, xbc, p, dtt, bias_row, bias_col, a_row, a_col, expand(0), expand(1))


def rope_tables(l_lat, rot_dim, nb, l_ctx):
    rows = l_lat // GRID_W
    row = jnp.repeat(jnp.arange(rows, dtype=F32), GRID_W)
    col = jnp.tile(jnp.arange(GRID_W, dtype=F32), rows)
    n_freq = rot_dim // 4
    inv_freq = ROPE_THETA ** (-jnp.arange(n_freq, dtype=F32) / n_freq)
    ang = jnp.concatenate([row[:, None] * inv_freq, col[:, None] * inv_freq], axis=-1)
    cos = jnp.concatenate([jnp.tile(jnp.cos(ang), (nb, 1)), jnp.ones((nb * l_ctx, rot_dim // 2), F32)], axis=0)
    sin = jnp.concatenate([jnp.tile(jnp.sin(ang), (nb, 1)), jnp.zeros((nb * l_ctx, rot_dim // 2), F32)], axis=0)
    return cos, sin


def _store_values_t(vt_ref, vt, nheads):
    dv = vt.shape[0] // nheads
    tail = (lax.broadcasted_iota(jnp.int32, (VT_PAD, vt.shape[1]), 0) == 0).astype(BF16)
    for h in range(nheads):
        vt_ref[h * (dv + VT_PAD):h * (dv + VT_PAD) + dv, :] = vt[h * dv:(h + 1) * dv].astype(BF16)
        vt_ref[h * (dv + VT_PAD) + dv:(h + 1) * (dv + VT_PAD), :] = tail


def _mla_prep_kernel(ql_ref, kvl_ref, kr_ref, qnw_ref, kvnw_ref, wq_ref, wk_ref, wvt_ref, c_ref, s1_ref, s2_ref,
                     q_ref, k_ref, vt_ref, *, nheads, scale):
    qn = _rms(ql_ref[...], qnw_ref[...]).astype(BF16)
    q = _dot(qn, wq_ref[...])
    ckv = _rms(kvl_ref[...], kvnw_ref[...]).astype(BF16)
    kn = _dot(ckv, wk_ref[...])
    _store_values_t(vt_ref, _dot_nt(wvt_ref[...], ckv), nheads)
    c, s1, s2 = c_ref[...], s1_ref[...], s2_ref[...]

    def rope(v):
        return v * c + pltpu.roll(v, 3 * LANE // 4, 1) * s1 + pltpu.roll(v, LANE // 4, 1) * s2

    kr = rope(kr_ref[...]).astype(BF16)
    for h in range(nheads):
        b0 = 2 * LANE * h
        q_ref[:, b0:b0 + LANE] = (q[:, b0:b0 + LANE] * scale).astype(BF16)
        q_ref[:, b0 + LANE:b0 + 2 * LANE] = (rope(q[:, b0 + LANE:b0 + 2 * LANE]) * scale).astype(BF16)
        k_ref[:, b0:b0 + LANE] = kn[:, LANE * h:LANE * (h + 1)].astype(BF16)
        k_ref[:, b0 + LANE:b0 + 2 * LANE] = kr


def mla_prep(p, blk_q, blk_kv, blk_kr, q_norm_w, kv_norm_w, w_qb, w_kvb, cos, sin, seq_blk):
    t = p.shape[0]
    nh = MLA_HEADS
    qk = MLA_NOPE + MLA_ROPE
    ql, kvl = q_norm_w.shape[0], kv_norm_w.shape[0]
    assert MLA_NOPE == LANE and MLA_ROPE == LANE // 2 and kvl == LANE
    dv = w_kvb.shape[1] // nh - MLA_NOPE
    wq = w_qb.reshape(ql, nh, qk)
    wq = jnp.concatenate([wq, jnp.zeros((ql, nh, 2 * LANE - qk), w_qb.dtype)], axis=-1).reshape(ql, nh * 2 * LANE)
    wkv = w_kvb.reshape(kvl, nh, MLA_NOPE + dv)
    wk = wkv[:, :, :MLA_NOPE].reshape(kvl, nh * MLA_NOPE)
    wvt = wkv[:, :, MLA_NOPE:].reshape(kvl, nh * dv).T
    half = MLA_ROPE // 2
    zeros = jnp.zeros((t, half), F32)
    c_tab = jnp.concatenate([cos, cos, jnp.ones((t, 2 * half), F32)], axis=1)
    s1_tab = jnp.concatenate([-sin, zeros, zeros, zeros], axis=1)
    s2_tab = jnp.concatenate([zeros, sin, zeros, zeros], axis=1)
    row = lambda w: pl.BlockSpec((TP, w), lambda i: (i, 0))
    full = lambda a: pl.BlockSpec(a.shape, lambda i: (0, 0))
    wq, wk, wvt = wq.astype(BF16), wk.astype(BF16), wvt.astype(BF16)
    qnw, kvnw = q_norm_w.reshape(1, ql).astype(F32), kv_norm_w.reshape(1, kvl).astype(F32)
    return pl.pallas_call(
        functools.partial(_mla_prep_kernel, nheads=nh, scale=float(qk) ** -0.5 * LOG2E),
        out_shape=(jax.ShapeDtypeStruct((t, nh * 2 * LANE), BF16), jax.ShapeDtypeStruct((t, nh * 2 * LANE), BF16),
                   jax.ShapeDtypeStruct((nh * (dv + VT_PAD), t), BF16)),
        grid=(t // TP,),
        in_specs=[pl.BlockSpec((TP, ql), lambda i: (i, blk_q)),
                  pl.BlockSpec((TP, kvl), lambda i: (i, blk_kv)),
                  pl.BlockSpec((TP, LANE), lambda i: (i, blk_kr)),
                  full(qnw), full(kvnw), full(wq), full(wk), full(wvt), row(LANE), row(LANE), row(LANE)],
        out_specs=(row(nh * 2 * LANE), pl.BlockSpec((TP, nh * 2 * LANE), lambda i: (seq_blk(i), 0)),
                   pl.BlockSpec((nh * (dv + VT_PAD), TP), lambda i: (0, seq_blk(i)))),
        compiler_params=_cp(("parallel",), 32),
        name="mla_prep",
    )(p, p, p, qnw, kvnw, wq, wk, wvt, c_tab, s1_tab, s2_tab)


def _gqa_prep_kernel(pg_ref, qnw_ref, knw_ref, c_ref, s_ref, q_ref, k_ref, vt_ref, *, nq, nkv, scale):
    c, s = c_ref[...], s_ref[...]

    def norm_rope(v, w):
        n = _rms(v, w)
        return n * c + pltpu.roll(n, LANE // 2, 1) * s

    for h in range(nq):
        q_ref[:, h * LANE:(h + 1) * LANE] = (norm_rope(pg_ref[:, h * LANE:(h + 1) * LANE], qnw_ref[...]) * scale).astype(BF16)
    for h in range(nkv):
        o = (nq + h) * LANE
        k_ref[:, h * LANE:(h + 1) * LANE] = norm_rope(pg_ref[:, o:o + LANE], knw_ref[...]).astype(BF16)
    o = (nq + nkv) * LANE
    _store_values_t(vt_ref, pg_ref[:, o:o + nkv * LANE].T, nkv)


def gqa_prep(p, blk, q_norm_w, k_norm_w, cos, sin, seq_blk):
    t = p.shape[0]
    nq, nkv = GQA_HEADS, GQA_KV_HEADS
    hd = q_norm_w.shape[0]
    assert hd == LANE
    w = (nq + 2 * nkv) * hd
    c_tab = jnp.concatenate([cos, cos], axis=1)
    s_tab = jnp.concatenate([-sin, sin], axis=1)
    row = lambda n: pl.BlockSpec((TP, n), lambda i: (i, 0))
    one = pl.BlockSpec((1, hd), lambda i: (0, 0))
    return pl.pallas_call(
        functools.partial(_gqa_prep_kernel, nq=nq, nkv=nkv, scale=float(hd) ** -0.5 * LOG2E),
        out_shape=(jax.ShapeDtypeStruct((t, nq * hd), BF16), jax.ShapeDtypeStruct((t, nkv * hd), BF16),
                   jax.ShapeDtypeStruct((nkv * (hd + VT_PAD), t), BF16)),
        grid=(t // TP,),
        in_specs=[pl.BlockSpec((TP, w), lambda i: (i, blk)), one, one, row(hd), row(hd)],
        out_specs=(row(nq * hd), pl.BlockSpec((TP, nkv * hd), lambda i: (seq_blk(i), 0)),
                   pl.BlockSpec((nkv * (hd + VT_PAD), TP), lambda i: (0, seq_blk(i)))),
        compiler_params=_cp(("parallel",), 32),
        name="gqa_prep",
    )(p, q_norm_w.reshape(1, hd).astype(F32), k_norm_w.reshape(1, hd).astype(F32), c_tab, s_tab)


def _flash_kernel(*refs, rep, rows, dk, dv, tk, n_chunks, aliased):
    if aliased:
        refs = refs[1:]
    if n_chunks > 1:
        q_ref, k_ref, vt_ref, o_ref, m_scr, acc_scr, s_scr = refs
    else:
        q_ref, k_ref, vt_ref, o_ref, m_scr, acc_scr = refs
    tq = q_ref.shape[0] // rows
    streams = [(h, r) for h in range(rows) for r in range(rep)]
    qs = [q_ref[h * tq:(h + 1) * tq, r * dk:(r + 1) * dk] for h, r in streams]

    def chunk_softmax(st, vt):
        mc = jnp.max(st, axis=0, keepdims=True)
        pt = jnp.exp2(st - mc).astype(BF16)
        return mc, _dot(vt, pt)

    def merge(s, mc, pv):
        m_old = m_scr[s]
        m_new = jnp.maximum(m_old, mc)
        m_scr[s] = m_new
        acc_scr[s] = jnp.exp2(m_old - m_new) * acc_scr[s] + jnp.exp2(mc - m_new) * pv

    m_scr[...] = jnp.full(m_scr.shape, M_INIT, F32)
    acc_scr[...] = jnp.zeros_like(acc_scr)
    if n_chunks == 1:
        for s, q in enumerate(qs):
            merge(s, *chunk_softmax(_dot_nt(k_ref[...], q), vt_ref[...]))
    else:
        for s, q in enumerate(qs):
            s_scr[s, 0] = _dot_nt(k_ref[0:tk, :], q)

        def step(j, cur, last=False):
            start = (lambda c: c * tk) if isinstance(j, int) else (lambda c: pl.multiple_of(c * tk, tk))
            off = start(j)
            for s, q in enumerate(qs):
                if not last:
                    s_scr[s, 1 - cur] = _dot_nt(k_ref[pl.ds(start(j + 1), tk), :], q)
                merge(s, *chunk_softmax(s_scr[s, cur], vt_ref[:, pl.ds(off, tk)]))

        def body(i, carry):
            step(2 * i, 0)
            step(2 * i + 1, 1)
            return carry

        lax.fori_loop(0, n_chunks // 2 - 1, body, 0)
        step(n_chunks - 2, 0)
        step(n_chunks - 1, 1, last=True)
    for s, (h, r) in enumerate(streams):
        out = acc_scr[s, 0:dv, :] / acc_scr[s, dv:dv + 1, :]
        o_ref[h * tq:(h + 1) * tq, r * dv:(r + 1) * dv] = out.T.astype(o_ref.dtype)


def _key_chunk(s_len):
    for tk in range(KEY_CHUNK_MAX, 0, -LANE):
        if s_len % tk == 0 and (s_len // tk) % 2 == 0:
            return tk
    raise ValueError(f"no key chunk for sequence length {s_len}")


def attention(q, k, vt, nb, l_lat, l_ctx, nl, hkv, rep, dk, dv, need_ctx):
    t = q.shape[0]
    s_len = l_ctx + l_lat
    tq = 512 if l_lat % 512 == 0 else TP
    tk = _key_chunk(s_len)
    rows = 2 // rep if l_lat % (tq * 2 // rep) == 0 else 1
    nq = l_lat // (tq * rows)
    cb = nl // l_ctx
    spb = s_len // l_ctx
    dvx = dv + VT_PAD
    scratch = lambda ns, n: [pltpu.VMEM((ns, 1, n), F32), pltpu.VMEM((ns, dvx, n), F32)]
    o = pl.pallas_call(
        functools.partial(_flash_kernel, rep=rep, rows=rows, dk=dk, dv=dv, tk=tk, n_chunks=s_len // tk, aliased=False),
        out_shape=jax.ShapeDtypeStruct((t, hkv * rep * dv), BF16),
        grid=(nb, hkv, nq),
        in_specs=[pl.BlockSpec((tq * rows, rep * dk), lambda b, g, i: (b * nq + i, g)),
                  pl.BlockSpec((s_len, dk), lambda b, g, i: (b, g)),
                  pl.BlockSpec((dvx, s_len), lambda b, g, i: (g, b))],
        out_specs=pl.BlockSpec((tq * rows, rep * dv), lambda b, g, i: (b * nq + i, g)),
        scratch_shapes=scratch(rows * rep, tq) + [pltpu.VMEM((rows * rep, 2, tk, tq), F32)],
        compiler_params=_cp(("parallel", "parallel", "arbitrary"), 56),
        name="attention_latent",
    )(q, k, vt)
    if not need_ctx:
        return o
    return pl.pallas_call(
        functools.partial(_flash_kernel, rep=rep, rows=1, dk=dk, dv=dv, tk=l_ctx, n_chunks=1, aliased=True),
        out_shape=jax.ShapeDtypeStruct((t, hkv * rep * dv), BF16),
        grid=(nb, hkv),
        in_specs=[pl.BlockSpec(memory_space=pl.ANY),
                  pl.BlockSpec((l_ctx, rep * dk), lambda b, g: (cb + b, g)),
                  pl.BlockSpec((l_ctx, dk), lambda b, g: (b * spb, g)),
                  pl.BlockSpec((dvx, l_ctx), lambda b, g: (g, b * spb))],
        out_specs=pl.BlockSpec((l_ctx, rep * dv), lambda b, g: (cb + b, g)),
        scratch_shapes=scratch(rep, l_ctx),
        input_output_aliases={0: 0},
        compiler_params=_cp(("parallel", "parallel"), 32),
        name="attention_context",
    )(o, q, k, vt)


def _gelu_tanh(x):
    return 0.5 * x * (1.0 + jnp.tanh(math.sqrt(2.0 / math.pi) * (x + 0.044715 * (x * x * x))))


def _outproj_kernel(x_ref, g_ref, ys5_ref, u_ref, d_ref, gw_ref, gb_ref, yf_ref, yb_ref, xs_ref, z_ref, dssd_ref,
                    nw_ref, ymla_ref, ygqa_ref, w_ref, o_ref):
    wd = ys5_ref.shape[1]
    gl = _gelu_tanh(ys5_ref[...].astype(F32) + d_ref[...] * u_ref[...])
    s5 = gl * _sigmoid(_dot(gl.astype(BF16), gw_ref[...]) + gb_ref[...])
    ssd = _rms((yf_ref[...] + yb_ref[...] + dssd_ref[...] * xs_ref[...]) * _silu(z_ref[...]), nw_ref[...])
    acc = _dot(s5.astype(BF16), w_ref[0:wd, :])
    acc += _dot(ssd.astype(BF16), w_ref[wd:2 * wd, :])
    acc += _dot(ymla_ref[...], w_ref[2 * wd:3 * wd, :])
    acc += _dot(ygqa_ref[...], w_ref[3 * wd:4 * wd, :])
    o_ref[...] = x_ref[...] + g_ref[...] * acc


def out_projection(x, nblk, row_of_block, mod4, ys5, p, u_blk, z_blk, d_s5, glu_w, glu_b, yf, yb, xbc, d_ssd, nw_ssd,
                   ymla, ygqa, w_out, layer):
    t, d = x.shape
    wd = ys5.shape[1]
    row = lambda n: pl.BlockSpec((TM, n), lambda i: (i, 0))
    one = lambda n: pl.BlockSpec((1, n), lambda i: (0, 0))
    vec = lambda v: v.reshape(1, wd).astype(F32)
    return pl.pallas_call(
        _outproj_kernel,
        out_shape=jax.ShapeDtypeStruct((t, d), F32),
        grid=(nblk,),
        in_specs=[row(d),
                  pl.BlockSpec((None, None, 1, d), lambda i: (row_of_block(i), 5, 0, 0)),
                  row(wd),
                  pl.BlockSpec((TM, wd), lambda i: (i, u_blk)),
                  one(wd),
                  pl.BlockSpec((wd, wd), lambda i: (0, 0)),
                  one(wd),
                  row(wd), row(wd), row(wd),
                  pl.BlockSpec((TM, wd), lambda i: (i, z_blk)),
                  one(wd), one(wd),
                  row(wd), row(wd),
                  pl.BlockSpec((None, d, d), lambda i: (layer, 0, 0))],
        out_specs=row(d),
        compiler_params=_cp(("parallel",), 48),
        name="out_projection",
    )(x, mod4, ys5, p, vec(d_s5), glu_w.astype(BF16), vec(glu_b), yf, yb, xbc, p,
      vec(jnp.repeat(d_ssd.astype(F32), wd // d_ssd.shape[0])), vec(nw_ssd), ymla, ygqa, w_out)


COL_XBC, COL_GQA, COL_S5, COL_Z, COL_QL, COL_KVL, COL_KR, COL_DT = 0, 1024, 2048, 2560, 3072, 3456, 3584, 3712
N_IN = 3840


def _reorder_w_in(w_in, w_s5, w_ssd, xbc_w, n_dt, ql, kvl, rope, gqa_w):
    d = w_in.shape[0]
    o_ssd = w_s5
    o_mla = o_ssd + w_ssd + xbc_w + n_dt
    o_gqa = o_mla + ql + kvl + rope
    z = lambda n: jnp.zeros((d, n), w_in.dtype)
    parts = [w_in[:, o_ssd + w_ssd:o_ssd + w_ssd + xbc_w],
             w_in[:, o_gqa:o_gqa + gqa_w],
             w_in[:, 0:w_s5],
             w_in[:, o_ssd:o_ssd + w_ssd],
             w_in[:, o_mla:o_mla + ql],
             w_in[:, o_mla + ql:o_mla + ql + kvl],
             w_in[:, o_mla + ql + kvl:o_mla + ql + kvl + rope], z(LANE - rope),
             w_in[:, o_ssd + w_ssd + xbc_w:o_ssd + w_ssd + xbc_w + n_dt], z(LANE - n_dt)]
    out = jnp.concatenate(parts, axis=1)
    assert out.shape[1] == N_IN
    return out.astype(BF16)


def kernel(x, c, ctx, c_ctx, w_ada, b_ada, norm_w, ffn_w1, ffn_w3, ffn_w2, w_in, w_out, s5_lam_re, s5_lam_im, s5_log_dt, s5_b_re, s5_b_im, s5_c_re, s5_c_im, s5_d, s5_glu_w, s5_glu_b, ssd_conv_w, ssd_conv_b, ssd_dt_bias, ssd_a_log, ssd_d, ssd_norm_w, mla_q_norm_w, mla_w_qb, mla_kv_norm_w, mla_w_kvb, gqa_q_norm_w, gqa_k_norm_w, norm_f):
    nb, l_lat, d = x.shape
    l_ctx = ctx.shape[1]
    depth = w_ada.shape[0]
    nl, nc = nb * l_lat, nb * l_ctx
    t = nl + nc
    assert l_ctx == TP and nc == TM and l_lat % TM == 0 and l_lat % GRID_W == 0 and nb < SUBLANE
    assert SUBLANE % nb == 0 and (l_ctx // S5_CHUNK * nb) % SUBLANE == 0 and (l_lat // S5_CHUNK * nb) % SUBLANE == 0
    w_s5 = s5_d.shape[1]
    w_ssd = ssd_norm_w.shape[1]
    xbc_w = ssd_conv_w.shape[2]
    nheads = ssd_a_log.shape[2]
    ql, kvl = mla_q_norm_w.shape[1], mla_kv_norm_w.shape[1]
    gqa_hd = gqa_q_norm_w.shape[1]
    gqa_w = (GQA_HEADS + 2 * GQA_KV_HEADS) * gqa_hd
    assert (w_s5, w_ssd, xbc_w, ql, kvl, gqa_w) == (512, 512, 1024, 384, 128, 1024)

    xs = jnp.concatenate([x.reshape(nl, d), ctx.reshape(nc, d)], axis=0)
    act_in = jnp.concatenate([c, c_ctx[None, :], jnp.zeros((SUBLANE - nb - 1, d), F32)], axis=0)
    mod = ada_modulation(act_in, w_ada, b_ada)
    blocks_per_batch = l_lat // TM
    n_lat_blk = nl // TM
    row_of_block = lambda i: jnp.where(i < n_lat_blk, i // blocks_per_batch, nb)
    cos_m, sin_m = rope_tables(l_lat, MLA_ROPE, nb, l_ctx)
    cos_g, sin_g = rope_tables(l_lat, gqa_hd, nb, l_ctx)
    lat_tp, seq_tp = l_lat // TP, (l_ctx + l_lat) // TP
    seq_blk = lambda i: jnp.where(i < nl // TP, (i // lat_tp) * seq_tp + 1 + i % lat_tp, (i - nl // TP) * seq_tp)

    w1, w3, w2, w_out_b = ffn_w1.astype(BF16), ffn_w3.astype(BF16), ffn_w2.astype(BF16), w_out.astype(BF16)
    for l in range(depth):
        need_ctx = l < depth - 1
        mod4 = mod[l].reshape(SUBLANE, N_MOD, 1, d)
        nblk_all = t // TM
        nblk_tail = nblk_all if need_ctx else n_lat_blk
        xs = half_ffn(xs, nblk_all, row_of_block, mod4, 0, norm_w[l, 0], w1, w3, w2, l, 0)
        w_in_r = _reorder_w_in(w_in[l], w_s5, w_ssd, xbc_w, 2 * nheads, ql, kvl, MLA_ROPE, gqa_w)
        p = in_projection(xs, nblk_all, row_of_block, mod4, norm_w[l, 1], w_in_r)

        tables = s5_tables(s5_lam_re[l], s5_lam_im[l], s5_log_dt[l], s5_b_re[l], s5_b_im[l], s5_c_re[l], s5_c_im[l])
        ys5 = s5_scan(p, nl, nb, l_lat, l_ctx, COL_S5 // 512, tables)

        xbc = ssd_conv(p, COL_XBC // xbc_w, ssd_conv_w[l], ssd_conv_b[l], l_lat, nl)
        dtt = p[:, COL_DT:COL_DT + 2 * nheads].T
        yf, yb = ssd_scan(xbc, p, COL_DT // LANE, dtt, ssd_dt_bias[l], ssd_a_log[l], nb, l_lat, l_ctx, nl)

        qm, km, vtm = mla_prep(p, COL_QL // ql, COL_KVL // kvl, COL_KR // LANE, mla_q_norm_w[l], mla_kv_norm_w[l],
                               mla_w_qb[l], mla_w_kvb[l], cos_m, sin_m, seq_blk)
        ymla = attention(qm, km, vtm, nb, l_lat, l_ctx, nl, MLA_HEADS, 1, 2 * LANE,
                         mla_w_kvb.shape[2] // MLA_HEADS - MLA_NOPE, need_ctx)
        qg, kg, vtg = gqa_prep(p, COL_GQA // gqa_w, gqa_q_norm_w[l], gqa_k_norm_w[l], cos_g, sin_g, seq_blk)
        ygqa = attention(qg, kg, vtg, nb, l_lat, l_ctx, nl, GQA_KV_HEADS, GQA_HEADS // GQA_KV_HEADS, gqa_hd, gqa_hd, need_ctx)

        xs = out_projection(xs, nblk_tail, row_of_block, mod4, ys5, p, COL_S5 // 512, COL_Z // 512, s5_d[l], s5_glu_w[l],
                            s5_glu_b[l], yf, yb, xbc, ssd_d[l], ssd_norm_w[l], ymla, ygqa, w_out_b, l)
        xs = half_ffn(xs, nblk_tail, row_of_block, mod4, 6, norm_w[l, 2], w1, w3, w2, l, 1,
                      final_w=None if need_ctx else norm_f)
    return xs.reshape(nb, l_lat, d)
```
